```python
import numpy as np
import jax, jax.numpy as jnp
from jax import lax

D_MODEL = 1024
BATCH = 8
SEQ = 2048
DEPTH = 4
DEC_BATCH = 2
DEC_SEQ = 16384
PAST_LEN = 128

D_MIX = D_MODEL
N_MIXERS = 4
D_BRANCH = D_MIX // N_MIXERS
N_GROUPS = 4
D_GROUP = D_BRANCH // N_GROUPS
POOL_WINDOWS = (2, 4, 8, 16)
CHUNK = 128
GRID_W = 64
NA_KH_MAX = 8
NA_KW = 16
N_IN_SLICES = 11
D_IN = N_IN_SLICES * D_BRANCH
RMS_EPS = 1e-6
LN_EPS = 1e-5

kernel_name = "hybrid_parallel_group_encoder"


def rmsnorm(x, g):
    xf = x.astype(jnp.float32)
    y = xf * lax.rsqrt(jnp.mean(xf * xf, axis=-1, keepdims=True) + RMS_EPS)
    return (y * g.astype(jnp.float32)).astype(x.dtype)


def pool_mixer(a, w_pool, pool_scale):
    B, L, _ = a.shape
    ag = a.astype(jnp.float32).reshape(B, L, N_GROUPS, D_GROUP)
    cs = jnp.concatenate([jnp.zeros((B, 1, N_GROUPS, D_GROUP), jnp.float32),
                          lax.cumsum(ag, axis=1)], axis=1)
    t = np.arange(L)
    outs = []
    for g, w in enumerate(POOL_WINDOWS):
        lo = np.clip(t - w // 2, 0, L)
        hi = np.clip(t - w // 2 + w, 0, L)
        cnt = (hi - lo).astype(np.float32)[None, :, None]
        cs_g = cs[:, :, g]
        mean = (jnp.take(cs_g, hi, axis=1) - jnp.take(cs_g, lo, axis=1)) / cnt
        outs.append(mean - ag[:, :, g])
    p = jnp.stack(outs, axis=2)
    y = jnp.einsum('blgc,gcd->blgd', p, w_pool.astype(jnp.float32))
    y = y.reshape(B, L, D_BRANCH) * pool_scale.astype(jnp.float32)
    return y.astype(a.dtype)


def sgu_mixer(u, v, sgu_norm_g, sgu_w, sgu_b):
    B, L, _ = u.shape
    u = jax.nn.gelu(u)
    vf = jax.nn.gelu(v).astype(jnp.float32)
    mu = jnp.mean(vf, axis=-1, keepdims=True)
    var = jnp.mean((vf - mu) ** 2, axis=-1, keepdims=True)
    vn = (vf - mu) * lax.rsqrt(var + LN_EPS) * sgu_norm_g.astype(jnp.float32)
    vc = vn.reshape(B, L // CHUNK, CHUNK, N_GROUPS, D_GROUP)
    s = jnp.einsum('hpq,bnqhc->bnphc', sgu_w.astype(jnp.float32), vc)
    s = s + sgu_b.astype(jnp.float32).T[:, :, None]
    return (u.astype(jnp.float32) * s.reshape(B, L, D_BRANCH)).astype(u.dtype)


def fourier_mixer(f, fnet_w):
    B, L, _ = f.shape
    ff = f.astype(jnp.float32).reshape(B, L, N_GROUPS, D_GROUP)
    z = jnp.fft.fft2(ff, axes=(1, 3), norm='ortho').real
    y = jnp.einsum('blgc,gcd->blgd', z, fnet_w.astype(jnp.float32))
    return y.reshape(B, L, D_BRANCH).astype(f.dtype)


def na_mixer(q, k, v, na_rpb):
    B, L, _ = q.shape
    rows = L // GRID_W
    kh = min(NA_KH_MAX, rows)
    scale = D_GROUP ** -0.5
    qg = q.reshape(B, rows, GRID_W, N_GROUPS, D_GROUP)
    kg = k.reshape(B, rows, GRID_W, N_GROUPS, D_GROUP)
    vg = v.reshape(B, rows, GRID_W, N_GROUPS, D_GROUP)
    r = np.arange(rows)
    rs = np.clip(r - kh // 2, 0, rows - kh)
    c = np.arange(GRID_W)
    cst = np.clip(c - NA_KW // 2, 0, GRID_W - NA_KW)
    col_idx = cst[:, None] + np.arange(NA_KW)[None, :]
    dc = col_idx - c[:, None]
    rpb_c = na_rpb[:, :, dc + NA_KW - 1]

    def row_block(args):
        q_row, r_i, rs_i = args
        k_rows = lax.dynamic_slice_in_dim(kg, rs_i, kh, axis=1)
        v_rows = lax.dynamic_slice_in_dim(vg, rs_i, kh, axis=1)
        k_sel = k_rows[:, :, col_idx]
        v_sel = v_rows[:, :, col_idx]
        dr = rs_i + jnp.arange(kh) - r_i
        bias = rpb_c[:, dr + NA_KH_MAX - 1]
        bias = jnp.transpose(bias, (0, 2, 1, 3))[None].astype(jnp.float32)
        s = jnp.einsum('bqhd,bkqwhd->bhqkw', q_row, k_sel,
                       preferred_element_type=jnp.float32) * scale + bias
        p = jax.nn.softmax(s.reshape(B, N_GROUPS, GRID_W, kh * NA_KW), axis=-1)
        p = p.reshape(B, N_GROUPS, GRID_W, kh, NA_KW).astype(v.dtype)
        return jnp.einsum('bhqkw,bkqwhd->bqhd', p, v_sel)

    out = lax.map(row_block, (jnp.moveaxis(qg, 1, 0),
                              jnp.asarray(r, jnp.int32), jnp.asarray(rs, jnp.int32)))
    return jnp.moveaxis(out, 0, 1).reshape(B, L, D_BRANCH)


def mixer_layer(x, c, norm_g, w_ada, b_ada, w_in, w_out, pool_w, pool_scale,
                sgu_norm_g, sgu_w, sgu_b, fnet_w, na_rpb):
    mod = jax.nn.silu(c) @ w_ada + b_ada
    shift, scl, gate = jnp.split(mod, 3, axis=-1)
    h = rmsnorm(x, norm_g) * (1.0 + scl[:, None]) + shift[:, None]
    z = h @ w_in
    (a_in, a_gate, b_u, b_v, b_gate, c_in, c_gate,
     d_q, d_k, d_v, d_gate) = jnp.split(z, N_IN_SLICES, axis=-1)
    ya = pool_mixer(a_in, pool_w, pool_scale) * jax.nn.silu(a_gate)
    yb = sgu_mixer(b_u, b_v, sgu_norm_g, sgu_w, sgu_b) * jax.nn.silu(b_gate)
    yc = fourier_mixer(c_in, fnet_w) * jax.nn.silu(c_gate)
    yd = na_mixer(d_q, d_k, d_v, na_rpb) * jax.nn.silu(d_gate)
    y = jnp.concatenate([ya, yb, yc, yd], axis=-1) @ w_out
    return x + gate[:, None] * y


def trunk(x, c, norm_g, w_ada, b_ada, w_in, w_out, pool_w, pool_scale,
          sgu_norm_g, sgu_w, sgu_b, fnet_w, na_rpb, final_norm_g):
    for l in range(DEPTH):
        x = mixer_layer(x, c, norm_g[l], w_ada[l], b_ada[l], w_in[l], w_out[l],
                        pool_w[l], pool_scale[l], sgu_norm_g[l], sgu_w[l], sgu_b[l],
                        fnet_w[l], na_rpb[l])
    return rmsnorm(x, final_norm_g)


def setup_inputs(seed: int = 0) -> dict:
    key = jax.random.key(seed)
    ks = jax.random.split(key, 17)
    f32 = jnp.float32
    n = lambda k, s: jax.random.normal(k, s, f32)
    return {
        "x_prompt": n(ks[0], (BATCH, SEQ, D_MODEL)),
        "x_sample": n(ks[1], (DEC_BATCH, DEC_SEQ, D_MODEL)),
        "c_prompt": n(ks[2], (BATCH, D_MODEL)),
        "c_sample": n(ks[3], (DEC_BATCH, D_MODEL)),
        "norm_g": 1.0 + 0.02 * n(ks[4], (DEPTH, D_MODEL)),
        "w_ada": n(ks[5], (DEPTH, D_MODEL, 3 * D_MODEL)) * (0.5 * D_MODEL ** -0.5),
        "b_ada": 0.02 * n(ks[6], (DEPTH, 3 * D_MODEL)),
        "w_in": n(ks[7], (DEPTH, D_MODEL, D_IN)) * D_MODEL ** -0.5,
        "w_out": n(ks[8], (DEPTH, D_MIX, D_MODEL)) * D_MIX ** -0.5,
        "pool_w": n(ks[9], (DEPTH, N_GROUPS, D_GROUP, D_GROUP)) * D_GROUP ** -0.5,
        "pool_scale": 1.0 + 0.02 * n(ks[10], (DEPTH, D_BRANCH)),
        "sgu_norm_g": 1.0 + 0.02 * n(ks[11], (DEPTH, D_BRANCH)),
        "sgu_w": n(ks[12], (DEPTH, N_GROUPS, CHUNK, CHUNK)) * CHUNK ** -0.5,
        "sgu_b": 1.0 + 0.02 * n(ks[13], (DEPTH, N_GROUPS, CHUNK)),
        "fnet_w": n(ks[14], (DEPTH, N_GROUPS, D_GROUP, D_GROUP)) * D_GROUP ** -0.5,
        "na_rpb": 0.1 * n(ks[15], (DEPTH, N_GROUPS, 2 * NA_KH_MAX - 1, 2 * NA_KW - 1)),
        "final_norm_g": 1.0 + 0.02 * n(ks[16], (D_MODEL,)),
    }


def reference(x_prompt, x_sample, c_prompt, c_sample, norm_g, w_ada, b_ada, w_in, w_out,
              pool_w, pool_scale, sgu_norm_g, sgu_w, sgu_b, fnet_w, na_rpb, final_norm_g):
    y_prompt = trunk(x_prompt, c_prompt, norm_g, w_ada, b_ada, w_in, w_out, pool_w, pool_scale,
                     sgu_norm_g, sgu_w, sgu_b, fnet_w, na_rpb, final_norm_g)
    y_sample = trunk(x_sample, c_sample, norm_g, w_ada, b_ada, w_in, w_out, pool_w, pool_scale,
                     sgu_norm_g, sgu_w, sgu_b, fnet_w, na_rpb, final_norm_g)
    return (y_prompt, y_sample)
```

```python
import functools

import numpy as np
import jax
import jax.numpy as jnp
from jax import lax
from jax.experimental import pallas as pl
from jax.experimental.pallas import tpu as pltpu

F32 = jnp.float32
BF16 = jnp.bfloat16
BF16_ROWS = 16

D_MODEL = 1024
DEPTH = 4
D_BRANCH = 256
N_GROUPS = 4
D_GROUP = 64
POOL_WINDOWS = (2, 4, 8, 16)
POOL_HALO = BF16_ROWS
CHUNK = 128
GRID_W = 64
NA_KH = 8
NA_KW = 16
N_IN_SLICES = 11
D_IN = N_IN_SLICES * D_BRANCH
RMS_EPS = 1e-6
LN_EPS = 1e-5
NEG_BIG = -1e30

A_IN, A_GATE, B_U, B_V, B_GATE, C_IN, C_GATE, D_Q, D_K, D_V, D_GATE = range(N_IN_SLICES)

NA_TR = 4
NA_KROWS = NA_TR + NA_KH
NA_TQ = NA_TR * GRID_W
NA_CQ = 16
NA_NCB = GRID_W // NA_CQ
NA_CK = 2 * NA_KW
NA_BQ = NA_TR * NA_CQ
NA_BK = NA_KROWS * NA_CK
C_PAD = 16
DFT_T = BF16_ROWS
DFT_K = 256

VMEM_LIMIT = 48 * 1024 * 1024


def _params(n_axes, vmem=VMEM_LIMIT):
    return pltpu.CompilerParams(dimension_semantics=("arbitrary",) * n_axes,
                                vmem_limit_bytes=vmem)


def _silu(x):
    return x / (1.0 + jnp.exp(-x))


def _gelu(x):
    return x * (0.5 * (1.0 + jnp.tanh(np.sqrt(2.0 / np.pi) * (x + 0.044715 * (x * x * x)))))


def _rms(x):
    return x * lax.rsqrt(jnp.mean(x * x, axis=-1, keepdims=True) + RMS_EPS)


def _tile_variant(i, n):
    return jnp.where(i == 0, 0, jnp.where(i == n - 1, 2, 1))


def _adaln_kernel(c_ref, w_ref, b_ref, o_ref):
    s = _silu(c_ref[...])
    o_ref[...] = jnp.dot(s, w_ref[...], precision=lax.Precision.HIGHEST,
                         preferred_element_type=F32) + b_ref[...]


def _adaln(c_all, w_ada, b_ada):
    return pl.pallas_call(
        _adaln_kernel,
        grid=(DEPTH, 3),
        in_specs=[pl.BlockSpec((C_PAD, D_MODEL), lambda l, n: (0, 0)),
                  pl.BlockSpec((None, D_MODEL, D_MODEL), lambda l, n: (l, 0, n)),
                  pl.BlockSpec((None, 1, D_MODEL), lambda l, n: (l, 0, n))],
        out_specs=pl.BlockSpec((None, C_PAD, D_MODEL), lambda l, n: (l, 0, n)),
        out_shape=jax.ShapeDtypeStruct((DEPTH, C_PAD, 3 * D_MODEL), F32),
        compiler_params=_params(2), name="adaln",
    )(c_all, w_ada, b_ada.reshape(DEPTH, 1, 3 * D_MODEL))


def _inproj_kernel(x_ref, g_ref, shift_ref, scl_ref, w_ref, z_ref):
    h = (_rms(x_ref[...]) * g_ref[...]) * (1.0 + scl_ref[...]) + shift_ref[...]
    h = h.astype(BF16)
    for s in range(N_IN_SLICES):
        z_ref[s] = jnp.dot(h, w_ref[:, s * D_BRANCH:(s + 1) * D_BRANCH],
                           preferred_element_type=F32).astype(BF16)


def _inproj(x, g, shift, scl, w_bf, tm=512):
    B, L, _ = x.shape
    vec = pl.BlockSpec((None, 1, D_MODEL), lambda b, i: (b, 0, 0))
    return pl.pallas_call(
        _inproj_kernel,
        grid=(B, L // tm),
        in_specs=[pl.BlockSpec((None, tm, D_MODEL), lambda b, i: (b, i, 0)),
                  pl.BlockSpec((1, D_MODEL), lambda b, i: (0, 0)),
                  vec, vec,
                  pl.BlockSpec((D_MODEL, D_IN), lambda b, i: (0, 0))],
        out_specs=pl.BlockSpec((N_IN_SLICES, None, tm, D_BRANCH), lambda b, i: (0, b, i, 0)),
        out_shape=jax.ShapeDtypeStruct((N_IN_SLICES, B, L, D_BRANCH), BF16),
        compiler_params=_params(2), name="inproj",
    )(x, g.reshape(1, D_MODEL), shift, scl, w_bf)


def _slab(s, t):
    return pl.BlockSpec((None, None, t, D_BRANCH), lambda b, i: (s, b, i, 0))


@functools.lru_cache(maxsize=None)
def _pool_tables(tp):
    r = np.arange(CHUNK)[:, None]
    c = np.arange(CHUNK + 2 * POOL_HALO)[None, :]
    band = np.stack([(c >= r + POOL_HALO - w // 2) & (c < r + POOL_HALO + w // 2)
                     for w in POOL_WINDOWS]).astype(np.float32)
    t = np.arange(tp)[:, None]
    w = np.repeat(np.array(POOL_WINDOWS), D_GROUP)[None, :]
    head = np.minimum(t + w // 2, tp + w) - np.maximum(t - w // 2, 0)
    tail = np.minimum(t + w // 2, tp) - np.maximum(t - w // 2, -w)
    inv = np.stack([1.0 / head, 1.0 / np.broadcast_to(w, head.shape), 1.0 / tail])
    return band, inv.astype(np.float32)


def _pool_kernel(a_ref, prev_ref, next_ref, gate_ref, band_ref, inv_ref, w_ref, scale_ref,
                 o_ref, ext_ref, *, tp):
    i = pl.program_id(1)
    n = pl.num_programs(1)
    H = POOL_HALO
    zero = jnp.zeros((H, D_BRANCH), BF16)
    ext_ref[0:H, :] = jnp.where(i > 0, prev_ref[...], zero)
    ext_ref[H:H + tp, :] = a_ref[...]
    ext_ref[H + tp:2 * H + tp, :] = jnp.where(i < n - 1, next_ref[...], zero)
    group = lax.broadcasted_iota(jnp.int32, (CHUNK, D_BRANCH), 1) // D_GROUP
    for m in range(tp // CHUNK):
        rows = slice(m * CHUNK, (m + 1) * CHUNK)
        e = ext_ref[m * CHUNK:(m + 1) * CHUNK + 2 * H, :]
        s = jnp.dot(band_ref[0], e, preferred_element_type=F32)
        for g in range(1, N_GROUPS):
            s = jnp.where(group == g, jnp.dot(band_ref[g], e, preferred_element_type=F32), s)
        p = s * inv_ref[rows, :] - a_ref[rows, :].astype(F32)
        y = jnp.dot(p.astype(BF16), w_ref[...], preferred_element_type=F32) * scale_ref[...]
        o_ref[rows, :] = (y * _silu(gate_ref[rows, :].astype(F32))).astype(BF16)


def _pool(z, w_bd, scale, tp=512):
    _, B, L, _ = z.shape
    H = POOL_HALO
    nb = L // H
    nt = L // tp
    band, inv = _pool_tables(tp)
    halo_prev = pl.BlockSpec((None, None, H, D_BRANCH),
                             lambda b, i: (A_IN, b, jnp.maximum(i * (tp // H) - 1, 0), 0))
    halo_next = pl.BlockSpec((None, None, H, D_BRANCH),
                             lambda b, i: (A_IN, b, jnp.minimum((i + 1) * (tp // H), nb - 1), 0))
    return pl.pallas_call(
        functools.partial(_pool_kernel, tp=tp),
        grid=(B, nt),
        in_specs=[_slab(A_IN, tp), halo_prev, halo_next, _slab(A_GATE, tp),
                  pl.BlockSpec(band.shape, lambda b, i: (0, 0, 0)),
                  pl.BlockSpec((None, tp, D_BRANCH), lambda b, i: (_tile_variant(i, nt), 0, 0)),
                  pl.BlockSpec((D_BRANCH, D_BRANCH), lambda b, i: (0, 0)),
                  pl.BlockSpec((1, D_BRANCH), lambda b, i: (0, 0))],
        out_specs=pl.BlockSpec((None, tp, D_BRANCH), lambda b, i: (b, i, 0)),
        out_shape=jax.ShapeDtypeStruct((B, L, D_BRANCH), BF16),
        scratch_shapes=[pltpu.VMEM((tp + 2 * H, D_BRANCH), BF16)],
        compiler_params=_params(2), name="pool",
    )(z, z, z, z, jnp.asarray(band).astype(BF16), jnp.asarray(inv), w_bd,
      scale.reshape(1, D_BRANCH))


def _sgu_kernel(u_ref, v_ref, gate_ref, g_ref, w_ref, b_ref, o_ref, *, ts):
    u = _gelu(u_ref[...].astype(F32))
    v = _gelu(v_ref[...].astype(F32))
    mu = jnp.mean(v, axis=-1, keepdims=True)
    d = v - mu
    var = jnp.mean(d * d, axis=-1, keepdims=True)
    vn = (d * lax.rsqrt(var + LN_EPS) * g_ref[...]).astype(BF16)
    gate = _silu(gate_ref[...].astype(F32))
    head = lax.broadcasted_iota(jnp.int32, (CHUNK, D_BRANCH), 1) // D_GROUP
    for n in range(ts // CHUNK):
        rows = slice(n * CHUNK, (n + 1) * CHUNK)
        vc = vn[rows]
        s = jnp.dot(w_ref[0], vc, preferred_element_type=F32)
        for h in range(1, N_GROUPS):
            s = jnp.where(head == h, jnp.dot(w_ref[h], vc, preferred_element_type=F32), s)
        o_ref[rows, :] = ((u[rows] * (s + b_ref[...])) * gate[rows]).astype(BF16)


def _sgu(z, norm_g, w_bf, b_full, ts=512):
    _, B, L, _ = z.shape
    return pl.pallas_call(
        functools.partial(_sgu_kernel, ts=ts),
        grid=(B, L // ts),
        in_specs=[_slab(B_U, ts), _slab(B_V, ts), _slab(B_GATE, ts),
                  pl.BlockSpec((1, D_BRANCH), lambda b, i: (0, 0)),
                  pl.BlockSpec((N_GROUPS, CHUNK, CHUNK), lambda b, i: (0, 0, 0)),
                  pl.BlockSpec((CHUNK, D_BRANCH), lambda b, i: (0, 0))],
        out_specs=pl.BlockSpec((None, ts, D_BRANCH), lambda b, i: (b, i, 0)),
        out_shape=jax.ShapeDtypeStruct((B, L, D_BRANCH), BF16),
        compiler_params=_params(2), name="sgu",
    )(z, z, z, norm_g.reshape(1, D_BRANCH), w_bf, b_full)


def _dft_split(L):
    n2 = 128
    return L // n2, n2


def _dft_group(n1):
    return DFT_K // (2 * n1)


@functools.lru_cache(maxsize=None)
def _fnet_tables(L):
    n1, n2 = _dft_split(L)
    k1 = np.arange(n1, dtype=np.int64)
    n = np.arange(n2, dtype=np.int64)[:, None, None] + n2 * np.arange(n1, dtype=np.int64)[None, None, :]
    ang = 2.0 * np.pi * ((k1[None, :, None] * n) % L).astype(np.float64) / L
    gc, gs = np.cos(ang), np.sin(ang)
    gg = np.concatenate([np.concatenate([gc, -gs], axis=2),
                         np.concatenate([-gs, -gc], axis=2)], axis=1)
    jb = _dft_group(n1)
    grouped = np.zeros((n2 // jb, jb, 2 * n1, jb, 2 * n1))
    for j in range(jb):
        grouped[:, j, :, j, :] = gg[j::jb]
    gg = grouped.reshape(n2 // jb, DFT_K, DFT_K)
    kk = np.arange(n2, dtype=np.int64)
    ang2 = 2.0 * np.pi * ((kk[:, None] * kk[None, :]) % n2).astype(np.float64) / n2
    cs2 = np.concatenate([np.cos(ang2), np.sin(ang2)], axis=1)
    return gg.astype(np.float32), cs2.astype(np.float32)


@functools.lru_cache(maxsize=None)
def _channel_tables():
    c = np.arange(D_GROUP, dtype=np.int64)
    ang = 2.0 * np.pi * ((c[:, None] * c[None, :]) % D_GROUP).astype(np.float64) / D_GROUP
    eye = np.eye(N_GROUPS)
    return (np.kron(eye, np.cos(ang)).astype(np.float32),
            np.kron(eye, np.sin(ang)).astype(np.float32))


def _block_diag(w):
    eye = jnp.eye(N_GROUPS, dtype=w.dtype)
    out = jnp.einsum('...gcd,gh->...gchd', w, eye)
    return out.reshape(*w.shape[:-3], D_BRANCH, D_BRANCH)


def _fold_kernel(bdc_ref, bds_ref, w_ref, mc_ref, ms_ref):
    w = w_ref[...]
    hi = lax.Precision.HIGHEST
    mc_ref[...] = jnp.dot(bdc_ref[...], w, precision=hi, preferred_element_type=F32).astype(BF16)
    ms_ref[...] = jnp.dot(bds_ref[...], w, precision=hi, preferred_element_type=F32).astype(BF16)


def _fnet_fold(fnet_w_bd):
    bdc, bds = _channel_tables()
    full = pl.BlockSpec((D_BRANCH, D_BRANCH), lambda l: (0, 0))
    per = pl.BlockSpec((None, D_BRANCH, D_BRANCH), lambda l: (l, 0, 0))
    shp = jax.ShapeDtypeStruct((DEPTH, D_BRANCH, D_BRANCH), BF16)
    return pl.pallas_call(
        _fold_kernel, grid=(DEPTH,), in_specs=[full, full, per], out_specs=[per, per],
        out_shape=[shp, shp], compiler_params=_params(1), name="fnet_fold",
    )(jnp.asarray(bdc), jnp.asarray(bds), fnet_w_bd)


def _fnet1_kernel(x_ref, gg_ref, mc_ref, ms_ref, a_ref, *, n1):
    jb = _dft_group(n1)
    xt = pltpu.einshape("njc->jnc", x_ref[...].astype(F32))
    x2 = xt.reshape(DFT_T * n1, D_BRANCH).astype(BF16)
    u = jnp.dot(x2, mc_ref[...], preferred_element_type=F32)
    v = jnp.dot(x2, ms_ref[...], preferred_element_type=F32)
    for g in range(DFT_T // jb):
        parts = []
        for j in range(g * jb, (g + 1) * jb):
            parts += [u[j * n1:(j + 1) * n1], v[j * n1:(j + 1) * n1]]
        uv = jnp.concatenate(parts, axis=0).astype(BF16)
        a = jnp.dot(gg_ref[g], uv, preferred_element_type=F32)
        a_ref[g * jb:(g + 1) * jb] = a.reshape(jb, 2 * n1, D_BRANCH).astype(BF16)


def _fnet2_kernel(ar_ref, ai_ref, gate_ref, cs_ref, o_ref, *, norm):
    ar = pltpu.einshape("njc->jnc", ar_ref[...].astype(F32))
    ai = pltpu.einshape("njc->jnc", ai_ref[...].astype(F32))
    ys = []
    for j in range(DFT_T):
        a = jnp.concatenate([ar[j], ai[j]], axis=0).astype(BF16)
        ys.append(jnp.dot(cs_ref[...], a, preferred_element_type=F32))
    y = pltpu.einshape("jkc->kjc", jnp.stack(ys, axis=0))
    o_ref[...] = ((y * norm) * _silu(gate_ref[...].astype(F32))).astype(BF16)


def _fnet(z, mc, ms):
    _, B, L, _ = z.shape
    n1, n2 = _dft_split(L)
    T = DFT_T
    assert L == n1 * n2 and n1 % T == 0 and DFT_K % (2 * n1) == 0, L
    gg, cs2 = _fnet_tables(L)
    gg, cs2 = jnp.asarray(gg).astype(BF16), jnp.asarray(cs2).astype(BF16)
    full = pl.BlockSpec((D_BRANCH, D_BRANCH), lambda b, i: (0, 0))
    a = pl.pallas_call(
        functools.partial(_fnet1_kernel, n1=n1),
        grid=(B, n2 // T),
        in_specs=[pl.BlockSpec((None, None, n1, T, D_BRANCH), lambda b, i: (C_IN, b, 0, i, 0)),
                  pl.BlockSpec((T // _dft_group(n1), DFT_K, DFT_K), lambda b, i: (i, 0, 0)),
                  full, full],
        out_specs=pl.BlockSpec((None, T, 2 * n1, D_BRANCH), lambda b, i: (b, i, 0, 0)),
        out_shape=jax.ShapeDtypeStruct((B, n2, 2 * n1, D_BRANCH), BF16),
        compiler_params=_params(2), name="fnet_stage1",
    )(z.reshape(N_IN_SLICES, B, n1, n2, D_BRANCH), gg, mc, ms)

    nk = n1 // T
    y = pl.pallas_call(
        functools.partial(_fnet2_kernel, norm=float(1.0 / np.sqrt(L * D_GROUP))),
        grid=(B, nk),
        in_specs=[pl.BlockSpec((None, n2, T, D_BRANCH), lambda b, i: (b, 0, i, 0)),
                  pl.BlockSpec((None, n2, T, D_BRANCH), lambda b, i: (b, 0, nk + i, 0)),
                  pl.BlockSpec((None, None, n2, T, D_BRANCH), lambda b, i: (C_GATE, b, 0, i, 0)),
                  pl.BlockSpec((n2, 2 * n2), lambda b, i: (0, 0))],
        out_specs=pl.BlockSpec((None, n2, T, D_BRANCH), lambda b, i: (b, 0, i, 0)),
        out_shape=jax.ShapeDtypeStruct((B, n2, n1, D_BRANCH), BF16),
        compiler_params=_params(2), name="fnet_stage2",
    )(a, a, z.reshape(N_IN_SLICES, B, n2, n1, D_BRANCH), cs2)
    return y.reshape(B, L, D_BRANCH)


def _na_key_col0(cb):
    return int(np.clip(cb * NA_CQ - NA_KW // 2, 0, GRID_W - NA_CK))


@functools.lru_cache(maxsize=None)
def _na_tables():
    i = np.arange(NA_TR)[:, None]
    j = np.arange(NA_KROWS)[None, :]
    row_sel = np.zeros((3, NA_TR, NA_KROWS, 2 * NA_KH - 1), np.float32)
    row_ok = np.zeros((3, NA_TR, NA_KROWS), bool)
    for var, (base, rel) in enumerate(((0, np.zeros(NA_TR, int)), (-NA_TR, np.arange(NA_TR)),
                                       (-NA_KH, np.full(NA_TR, NA_KROWS - NA_KH)))):
        ok = (j >= rel[:, None]) & (j < rel[:, None] + NA_KH)
        dr = base + j - i + NA_KH - 1
        for ii, jj in zip(*np.nonzero(ok)):
            row_sel[var, ii, jj, dr[ii, jj]] = 1.0
        row_ok[var] = ok
    col_sel = np.zeros((NA_NCB, NA_CQ, NA_CK, 2 * NA_KW - 1), np.float32)
    col_ok = np.zeros((NA_NCB, NA_CQ, NA_CK), bool)
    for cb in range(NA_NCB):
        for cq in range(NA_CQ):
            c = cb * NA_CQ + cq
            cst = int(np.clip(c - NA_KW // 2, 0, GRID_W - NA_KW))
            for kl in range(NA_CK):
                kc = _na_key_col0(cb) + kl
                if cst <= kc < cst + NA_KW:
                    col_sel[cb, cq, kl, kc - c + NA_KW - 1] = 1.0
                    col_ok[cb, cq, kl] = True
    ok = row_ok[:, None, :, None, :, None] & col_ok[None, :, None, :, None, :]
    mask = np.where(ok, 0.0, NEG_BIG).astype(np.float32).reshape(3, NA_NCB, 1, NA_BQ, NA_BK)
    return row_sel, col_sel, mask


def _na_bias(na_rpb):
    row_sel, col_sel, mask = _na_tables()
    hi = lax.Precision.HIGHEST
    b = jnp.einsum('vija,lhab->lvhijb', row_sel, na_rpb, precision=hi)
    b = jnp.einsum('lvhijb,cqkb->lvchiqjk', b, col_sel, precision=hi)
    b = b.reshape(DEPTH, 3, NA_NCB, N_GROUPS, NA_BQ, NA_BK) + mask
    return b.reshape(DEPTH, 3, NA_NCB, N_GROUPS * NA_BQ, NA_BK)


def _na_kernel(q_ref, k0_ref, k1_ref, k2_ref, v0_ref, v1_ref, v2_ref, gate_ref, bias_ref, o_ref,
               acc_ref):
    head = lax.broadcasted_iota(jnp.int32, (NA_BQ, D_BRANCH), 1) // D_GROUP
    k = jnp.concatenate([k0_ref[...], k1_ref[...], k2_ref[...]], axis=0).astype(F32)
    v = jnp.concatenate([v0_ref[...], v1_ref[...], v2_ref[...]], axis=0).astype(F32)
    for cb in range(NA_NCB):
        c0 = cb * NA_CQ
        kc0 = _na_key_col0(cb)
        q = jnp.concatenate([q_ref[i * GRID_W + c0:i * GRID_W + c0 + NA_CQ, :]
                             for i in range(NA_TR)], axis=0)
        q = q * jnp.asarray(D_GROUP ** -0.5, BF16)
        zero = jnp.zeros_like(q)
        qs = jnp.concatenate([jnp.where(head == h, q, zero) for h in range(N_GROUPS)], axis=0)
        kb = jnp.concatenate([k[j * GRID_W + kc0:j * GRID_W + kc0 + NA_CK]
                              for j in range(NA_KROWS)], axis=0).astype(BF16)
        vb = jnp.concatenate([v[j * GRID_W + kc0:j * GRID_W + kc0 + NA_CK]
                              for j in range(NA_KROWS)], axis=0).astype(BF16)
        s = lax.dot_general(qs, kb, (((1,), (1,)), ((), ())), preferred_element_type=F32)
        s = s + bias_ref[cb]
        m = jnp.max(s, axis=-1, keepdims=True)
        p = jnp.exp(s - m)
        denom = jnp.sum(p, axis=-1, keepdims=True)
        o = jnp.dot(p.astype(BF16), vb, preferred_element_type=F32) / denom
        out = o[0:NA_BQ]
        for h in range(1, N_GROUPS):
            out = jnp.where(head == h, o[h * NA_BQ:(h + 1) * NA_BQ], out)
        for i in range(NA_TR):
            acc_ref[i * GRID_W + c0:i * GRID_W + c0 + NA_CQ, :] = out[i * NA_CQ:(i + 1) * NA_CQ]
    o_ref[...] = (acc_ref[...] * _silu(gate_ref[...].astype(F32))).astype(BF16)


def _na(z, bias):
    _, B, L, _ = z.shape
    nt = L // NA_TQ
    nkb = NA_KROWS // NA_TR

    def kv(s, d):
        return pl.BlockSpec((None, None, NA_TQ, D_BRANCH),
                            lambda b, i: (s, b, jnp.clip(i - 1, 0, nt - nkb) + d, 0))

    return pl.pallas_call(
        _na_kernel,
        grid=(B, nt),
        in_specs=[_slab(D_Q, NA_TQ),
                  kv(D_K, 0), kv(D_K, 1), kv(D_K, 2), kv(D_V, 0), kv(D_V, 1), kv(D_V, 2),
                  _slab(D_GATE, NA_TQ),
                  pl.BlockSpec((None, NA_NCB, N_GROUPS * NA_BQ, NA_BK),
                               lambda b, i: (_tile_variant(i, nt), 0, 0, 0))],
        out_specs=pl.BlockSpec((None, NA_TQ, D_BRANCH), lambda b, i: (b, i, 0)),
        out_shape=jax.ShapeDtypeStruct((B, L, D_BRANCH), BF16),
        scratch_shapes=[pltpu.VMEM((NA_TQ, D_BRANCH), F32)],
        compiler_params=_params(2), name="na",
    )(z, z, z, z, z, z, z, z, bias)


def _outproj_kernel(ya_ref, yb_ref, yc_ref, yd_ref, x_ref, gate_ref, w_ref, fg_ref, o_ref, *, final):
    y = jnp.dot(ya_ref[...], w_ref[0:D_BRANCH, :], preferred_element_type=F32)
    for n, r in enumerate((yb_ref, yc_ref, yd_ref), start=1):
        y = y + jnp.dot(r[...], w_ref[n * D_BRANCH:(n + 1) * D_BRANCH, :],
                        preferred_element_type=F32)
    x = x_ref[...] + gate_ref[...] * y
    o_ref[...] = _rms(x) * fg_ref[...] if final else x


def _outproj(ys, x, gate, w_bf, final_g, final, tm=512):
    B, L, _ = x.shape
    yspec = pl.BlockSpec((None, tm, D_BRANCH), lambda b, i: (b, i, 0))
    xspec = pl.BlockSpec((None, tm, D_MODEL), lambda b, i: (b, i, 0))
    return pl.pallas_call(
        functools.partial(_outproj_kernel, final=final),
        grid=(B, L // tm),
        in_specs=[yspec, yspec, yspec, yspec, xspec,
                  pl.BlockSpec((None, 1, D_MODEL), lambda b, i: (b, 0, 0)),
                  pl.BlockSpec((D_MODEL, D_MODEL), lambda b, i: (0, 0)),
                  pl.BlockSpec((1, D_MODEL), lambda b, i: (0, 0))],
        out_specs=xspec,
        out_shape=jax.ShapeDtypeStruct((B, L, D_MODEL), F32),
        compiler_params=_params(2), name="outproj",
    )(*ys, x, gate, w_bf, final_g.reshape(1, D_MODEL))


def _trunk(x, mod, prep, norm_g, pool_scale, sgu_norm_g, final_norm_g):
    B = x.shape[0]
    for l in range(DEPTH):
        shift, scl, gate = (mod[l, :, n * D_MODEL:(n + 1) * D_MODEL].reshape(B, 1, D_MODEL)
                            for n in range(3))
        z = _inproj(x, norm_g[l], shift, scl, prep["w_in"][l])
        ya = _pool(z, prep["pool_w"][l], pool_scale[l])
        yb = _sgu(z, sgu_norm_g[l], prep["sgu_w"][l], prep["sgu_b"][l])
        yc = _fnet(z, prep["fnet_mc"][l], prep["fnet_ms"][l])
        yd = _na(z, prep["na_bias"][l])
        x = _outproj((ya, yb, yc, yd), x, gate, prep["w_out"][l], final_norm_g,
                     final=(l == DEPTH - 1))
    return x


def kernel(x_prompt, x_sample, c_prompt, c_sample, norm_g, w_ada, b_ada, w_in, w_out, pool_w, pool_scale, sgu_norm_g, sgu_w, sgu_b, fnet_w, na_rpb, final_norm_g):
    nb_p, nb_s = c_prompt.shape[0], c_sample.shape[0]
    c_all = jnp.concatenate(
        [c_prompt, c_sample, jnp.zeros((C_PAD - nb_p - nb_s, D_MODEL), F32)], axis=0)
    mod = _adaln(c_all, w_ada, b_ada)

    mc, ms = _fnet_fold(_block_diag(fnet_w))
    prep = {
        "w_in": w_in.astype(BF16),
        "w_out": w_out.astype(BF16),
        "pool_w": _block_diag(pool_w).astype(BF16),
        "sgu_w": sgu_w.astype(BF16),
        "sgu_b": jnp.repeat(jnp.swapaxes(sgu_b, 1, 2), D_GROUP, axis=2),
        "fnet_mc": mc,
        "fnet_ms": ms,
        "na_bias": _na_bias(na_rpb),
    }
    y_prompt = _trunk(x_prompt, mod[:, :nb_p], prep, norm_g, pool_scale, sgu_norm_g, final_norm_g)
    y_sample = _trunk(x_sample, mod[:, nb_p:nb_p + nb_s], prep, norm_g, pool_scale, sgu_norm_g,
                      final_norm_g)
    return (y_prompt, y_sample)
```

```python
import functools

import numpy as np
import jax
import jax.numpy as jnp
from jax import lax
from jax.experimental import pallas as pl
from jax.experimental.pallas import tpu as pltpu

F32 = jnp.float32
BF16 = jnp.bfloat16
BF16_ROWS = 16

D_MODEL = 1024
DEPTH = 4
D_BRANCH = 256
N_GROUPS = 4
D_GROUP = 64
POOL_WINDOWS = (2, 4, 8, 16)
POOL_HALO = BF16_ROWS
CHUNK = 128
GRID_W = 64
NA_KH = 8
NA_KW = 16
N_IN_SLICES = 11
D_IN = N_IN_SLICES * D_BRANCH
RMS_EPS = 1e-6
LN_EPS = 1e-5
NEG_BIG = -1e30

A_IN, A_GATE, B_U, B_V, B_GATE, C_IN, C_GATE, D_Q, D_K, D_V, D_GATE = range(N_IN_SLICES)
Z_SLICES = (C_IN, C_GATE, D_Q, D_K, D_V, D_GATE)
ZC_IN, ZC_GATE, ZD_Q, ZD_K, ZD_V, ZD_GATE = range(len(Z_SLICES))

NA_TR = 4
NA_KROWS = NA_TR + NA_KH
NA_TQ = NA_TR * GRID_W
NA_CQ = 16
NA_NCB = GRID_W // NA_CQ
NA_CK = 2 * NA_KW
NA_BQ = NA_TR * NA_CQ
NA_BK = NA_KROWS * NA_CK
C_PAD = 16
DFT_T = BF16_ROWS
DFT_K = 256
TM = 512

VMEM_LIMIT = 56 * 1024 * 1024


def _params(n_axes, vmem=VMEM_LIMIT):
    return pltpu.CompilerParams(dimension_semantics=("arbitrary",) * n_axes,
                                vmem_limit_bytes=vmem)


def _silu(x):
    return x / (1.0 + jnp.exp(-x))


def _gelu(x):
    return x * (0.5 * (1.0 + jnp.tanh(np.sqrt(2.0 / np.pi) * (x + 0.044715 * (x * x * x)))))


def _rms(x):
    return x * lax.rsqrt(jnp.mean(x * x, axis=-1, keepdims=True) + RMS_EPS)


def _tile_variant(i, n):
    return jnp.where(i == 0, 0, jnp.where(i == n - 1, 2, 1))


def _const(shape):
    zeros = (0,) * len(shape)
    return pl.BlockSpec(shape, lambda b, i: zeros)


def _rows(t, width):
    return pl.BlockSpec((None, t, width), lambda b, i: (b, i, 0))


def _slab(s, t):
    return pl.BlockSpec((None, None, t, D_BRANCH), lambda b, i: (s, b, i, 0))


def _adaln_kernel(c_ref, w_ref, b_ref, o_ref):
    s = _silu(c_ref[...])
    o_ref[...] = jnp.dot(s, w_ref[...], precision=lax.Precision.HIGHEST,
                         preferred_element_type=F32) + b_ref[...]


def _adaln(c_all, w_ada, b_ada):
    return pl.pallas_call(
        _adaln_kernel,
        grid=(DEPTH, 3),
        in_specs=[pl.BlockSpec((C_PAD, D_MODEL), lambda l, n: (0, 0)),
                  pl.BlockSpec((None, D_MODEL, D_MODEL), lambda l, n: (l, 0, n)),
                  pl.BlockSpec((None, 1, D_MODEL), lambda l, n: (l, 0, n))],
        out_specs=pl.BlockSpec((None, C_PAD, D_MODEL), lambda l, n: (l, 0, n)),
        out_shape=jax.ShapeDtypeStruct((DEPTH, C_PAD, 3 * D_MODEL), F32),
        compiler_params=_params(2), name="adaln",
    )(c_all, w_ada, b_ada.reshape(DEPTH, 1, 3 * D_MODEL))


@functools.lru_cache(maxsize=None)
def _pool_tables(tp):
    r = np.arange(CHUNK)[:, None]
    c = np.arange(CHUNK + 2 * POOL_HALO)[None, :]
    band = np.stack([(c >= r + POOL_HALO - w // 2) & (c < r + POOL_HALO + w // 2)
                     for w in POOL_WINDOWS]).astype(np.float32)
    t = np.arange(tp)[:, None]
    w = np.repeat(np.array(POOL_WINDOWS), D_GROUP)[None, :]
    head = np.minimum(t + w // 2, tp + w) - np.maximum(t - w // 2, 0)
    tail = np.minimum(t + w // 2, tp) - np.maximum(t - w // 2, -w)
    inv = np.stack([1.0 / head, 1.0 / np.broadcast_to(w, head.shape), 1.0 / tail])
    return band, inv.astype(np.float32)


def _mix_out(ys, w_ref):
    y = jnp.dot(ys[0], w_ref[0:D_BRANCH, :], preferred_element_type=F32)
    for n in range(1, len(ys)):
        y = y + jnp.dot(ys[n], w_ref[n * D_BRANCH:(n + 1) * D_BRANCH, :],
                        preferred_element_type=F32)
    return y


def _modulate(x, g_ref, shift_ref, scl_ref):
    return ((_rms(x) * g_ref[...]) * (1.0 + scl_ref[...]) + shift_ref[...]).astype(BF16)


def _project(h, w_ref, s):
    return jnp.dot(h, w_ref[:, s * D_BRANCH:(s + 1) * D_BRANCH], preferred_element_type=F32)


def _sgu_body(u, v, gate, g_ref, w_ref, b_ref, o_ref):
    t = u.shape[0]
    u = _gelu(u)
    v = _gelu(v)
    mu = jnp.mean(v, axis=-1, keepdims=True)
    d = v - mu
    var = jnp.mean(d * d, axis=-1, keepdims=True)
    vn = (d * lax.rsqrt(var + LN_EPS) * g_ref[...]).astype(BF16)
    gate = _silu(gate)
    head = lax.broadcasted_iota(jnp.int32, (CHUNK, D_BRANCH), 1) // D_GROUP
    for n in range(t // CHUNK):
        rows = slice(n * CHUNK, (n + 1) * CHUNK)
        vc = vn[rows]
        s = jnp.dot(w_ref[0], vc, preferred_element_type=F32)
        for h in range(1, N_GROUPS):
            s = jnp.where(head == h, jnp.dot(w_ref[h], vc, preferred_element_type=F32), s)
        o_ref[rows, :] = ((u[rows] * (s + b_ref[...])) * gate[rows]).astype(BF16)


def _pool_body(ext_ref, gate, band_ref, inv_ref, w_ref, scale_ref, o_ref):
    t = gate.shape[0]
    H = POOL_HALO
    gate = _silu(gate)
    group = lax.broadcasted_iota(jnp.int32, (CHUNK, D_BRANCH), 1) // D_GROUP
    for m in range(t // CHUNK):
        rows = slice(m * CHUNK, (m + 1) * CHUNK)
        e = ext_ref[m * CHUNK:(m + 1) * CHUNK + 2 * H, :]
        s = jnp.dot(band_ref[0], e, preferred_element_type=F32)
        for g in range(1, N_GROUPS):
            s = jnp.where(group == g, jnp.dot(band_ref[g], e, preferred_element_type=F32), s)
        p = s * inv_ref[rows, :] - e[H:H + CHUNK].astype(F32)
        y = jnp.dot(p.astype(BF16), w_ref[...], preferred_element_type=F32) * scale_ref[...]
        o_ref[rows, :] = (y * gate[rows]).astype(BF16)


def _layer_kernel(*refs, tm, first):
    n_stream = 2 if first else 12
    stream, rest = refs[:n_stream], refs[n_stream:]
    (g_ref, shift_ref, scl_ref, win_ref, sg_ref, sw_ref, sb_ref,
     band_ref, inv_ref, pw_ref, ps_ref) = rest[:11]
    outs = rest[11:]
    i = pl.program_id(1)
    n = pl.num_programs(1)
    H = POOL_HALO
    ext_ref, carry_ref = outs[-2:]
    slot_prev = (i + 1) % 2
    slot_this = i % 2

    @pl.when(i == 0)
    def _():
        carry_ref[slot_prev] = jnp.zeros((H, D_BRANCH), BF16)

    if first:
        x_ref, xh_ref = stream
        z_ref, ya_ref, yb_ref = outs[:3]
        x = x_ref[...]
        xh = xh_ref[...]
    else:
        ys, x_ref, ysh, xh_ref, gprev_ref, wout_ref = (stream[0:4], stream[4], stream[5:9],
                                                       stream[9], stream[10], stream[11])
        xo_ref, z_ref, ya_ref, yb_ref = outs[:4]
        y = _mix_out([jnp.concatenate([m[...], mh[...]], axis=0) for m, mh in zip(ys, ysh)],
                     wout_ref)
        x = x_ref[...] + gprev_ref[...] * y[0:tm]
        xh = xh_ref[...] + gprev_ref[...] * y[tm:tm + H]
        xo_ref[...] = x

    h = _modulate(x, g_ref, shift_ref, scl_ref)
    hh = _modulate(xh, g_ref, shift_ref, scl_ref)
    a_ext = _project(jnp.concatenate([h, hh], axis=0), win_ref, A_IN).astype(BF16)
    for k, s in enumerate(Z_SLICES):
        z_ref[k] = _project(h, win_ref, s).astype(BF16)
    _sgu_body(_project(h, win_ref, B_U), _project(h, win_ref, B_V), _project(h, win_ref, B_GATE),
              sg_ref, sw_ref, sb_ref, yb_ref)

    a = a_ext[0:tm]
    a_next = a_ext[tm:tm + H]
    ext_ref[0:H, :] = carry_ref[slot_prev]
    ext_ref[H:H + tm, :] = a
    ext_ref[H + tm:2 * H + tm, :] = jnp.where(i < n - 1, a_next, jnp.zeros_like(a_next))
    carry_ref[slot_this] = a[tm - H:tm]
    _pool_body(ext_ref, _project(h, win_ref, A_GATE), band_ref, inv_ref, pw_ref, ps_ref, ya_ref)


def _layer(ys, x, gate_prev, w_out, norm_g, shift, scl, w_in, sgu_g, sgu_w, sgu_b, pool_w,
           pool_scale, tm=TM):
    B, L, _ = x.shape
    first = ys is None
    H = POOL_HALO
    nt = L // tm
    nh = L // H
    band, inv = _pool_tables(tm)

    def halo(width):
        return pl.BlockSpec((None, H, width),
                            lambda b, i: (b, jnp.minimum((i + 1) * (tm // H), nh - 1), 0))

    vec = pl.BlockSpec((None, 1, D_MODEL), lambda b, i: (b, 0, 0))
    if first:
        stream_specs = [_rows(tm, D_MODEL), halo(D_MODEL)]
        stream_args = [x, x]
    else:
        stream_specs = ([_rows(tm, D_BRANCH)] * 4 + [_rows(tm, D_MODEL)] + [halo(D_BRANCH)] * 4
                        + [halo(D_MODEL), vec, _const((D_MODEL, D_MODEL))])
        stream_args = [*ys, x, *ys, x, gate_prev, w_out]
    param_specs = [_const((1, D_MODEL)), vec, vec, _const((D_MODEL, D_IN)),
                   _const((1, D_BRANCH)), _const((N_GROUPS, CHUNK, CHUNK)), _const((CHUNK, D_BRANCH)),
                   _const(band.shape),
                   pl.BlockSpec((None, tm, D_BRANCH), lambda b, i: (_tile_variant(i, nt), 0, 0)),
                   _const((D_BRANCH, D_BRANCH)), _const((1, D_BRANCH))]
    param_args = [norm_g.reshape(1, D_MODEL), shift, scl, w_in,
                  sgu_g.reshape(1, D_BRANCH), sgu_w, sgu_b,
                  jnp.asarray(band).astype(BF16), jnp.asarray(inv), pool_w,
                  pool_scale.reshape(1, D_BRANCH)]
    nz = len(Z_SLICES)
    out_specs = [pl.BlockSpec((nz, None, tm, D_BRANCH), lambda b, i: (0, b, i, 0)),
                 _rows(tm, D_BRANCH), _rows(tm, D_BRANCH)]
    out_shape = [jax.ShapeDtypeStruct((nz, B, L, D_BRANCH), BF16),
                 jax.ShapeDtypeStruct((B, L, D_BRANCH), BF16),
                 jax.ShapeDtypeStruct((B, L, D_BRANCH), BF16)]
    if not first:
        out_specs = [_rows(tm, D_MODEL)] + out_specs
        out_shape = [jax.ShapeDtypeStruct((B, L, D_MODEL), F32)] + out_shape
    res = pl.pallas_call(
        functools.partial(_layer_kernel, tm=tm, first=first),
        grid=(B, nt),
        in_specs=stream_specs + param_specs,
        out_specs=out_specs,
        out_shape=out_shape,
        scratch_shapes=[pltpu.VMEM((tm + 2 * H, D_BRANCH), BF16),
                        pltpu.VMEM((2, H, D_BRANCH), BF16)],
        compiler_params=_params(2), name="layer_first" if first else "layer",
    )(*stream_args, *param_args)
    return (x, *res) if first else tuple(res)


def _dft_split(L):
    n2 = 128
    return L // n2, n2


def _dft_group(n1):
    return DFT_K // (2 * n1)


@functools.lru_cache(maxsize=None)
def _fnet_tables(L):
    n1, n2 = _dft_split(L)
    k1 = np.arange(n1, dtype=np.int64)
    n = np.arange(n2, dtype=np.int64)[:, None, None] + n2 * np.arange(n1, dtype=np.int64)[None, None, :]
    ang = 2.0 * np.pi * ((k1[None, :, None] * n) % L).astype(np.float64) / L
    gc, gs = np.cos(ang), np.sin(ang)
    gg = np.concatenate([np.concatenate([gc, -gs], axis=2),
                         np.concatenate([-gs, -gc], axis=2)], axis=1)
    jb = _dft_group(n1)
    grouped = np.zeros((n2 // jb, jb, 2 * n1, jb, 2 * n1))
    for j in range(jb):
        grouped[:, j, :, j, :] = gg[j::jb]
    gg = grouped.reshape(n2 // jb, DFT_K, DFT_K)
    kk = np.arange(n2, dtype=np.int64)
    ang2 = 2.0 * np.pi * ((kk[:, None] * kk[None, :]) % n2).astype(np.float64) / n2
    cs2 = np.concatenate([np.cos(ang2), np.sin(ang2)], axis=1)
    return gg.astype(np.float32), cs2.astype(np.float32)


@functools.lru_cache(maxsize=None)
def _channel_tables():
    c = np.arange(D_GROUP, dtype=np.int64)
    ang = 2.0 * np.pi * ((c[:, None] * c[None, :]) % D_GROUP).astype(np.float64) / D_GROUP
    eye = np.eye(N_GROUPS)
    return (np.kron(eye, np.cos(ang)).astype(np.float32),
            np.kron(eye, np.sin(ang)).astype(np.float32))


def _block_diag(w):
    eye = jnp.eye(N_GROUPS, dtype=w.dtype)
    out = jnp.einsum('...gcd,gh->...gchd', w, eye)
    return out.reshape(*w.shape[:-3], D_BRANCH, D_BRANCH)


def _fold_kernel(bdc_ref, bds_ref, w_ref, mc_ref, ms_ref):
    w = w_ref[...]
    hi = lax.Precision.HIGHEST
    mc_ref[...] = jnp.dot(bdc_ref[...], w, precision=hi, preferred_element_type=F32).astype(BF16)
    ms_ref[...] = jnp.dot(bds_ref[...], w, precision=hi, preferred_element_type=F32).astype(BF16)


def _fnet_fold(fnet_w_bd):
    bdc, bds = _channel_tables()
    full = pl.BlockSpec((D_BRANCH, D_BRANCH), lambda l: (0, 0))
    per = pl.BlockSpec((None, D_BRANCH, D_BRANCH), lambda l: (l, 0, 0))
    shp = jax.ShapeDtypeStruct((DEPTH, D_BRANCH, D_BRANCH), BF16)
    return pl.pallas_call(
        _fold_kernel, grid=(DEPTH,), in_specs=[full, full, per], out_specs=[per, per],
        out_shape=[shp, shp], compiler_params=_params(1), name="fnet_fold",
    )(jnp.asarray(bdc), jnp.asarray(bds), fnet_w_bd)


def _fnet1_kernel(x_ref, gg_ref, mc_ref, ms_ref, a_ref, *, n1):
    jb = _dft_group(n1)
    xt = jnp.swapaxes(x_ref[...].astype(F32), 0, 1)
    x2 = xt.reshape(DFT_T * n1, D_BRANCH).astype(BF16)
    u = jnp.dot(x2, mc_ref[...], preferred_element_type=F32)
    v = jnp.dot(x2, ms_ref[...], preferred_element_type=F32)
    for g in range(DFT_T // jb):
        parts = []
        for j in range(g * jb, (g + 1) * jb):
            parts += [u[j * n1:(j + 1) * n1], v[j * n1:(j + 1) * n1]]
        uv = jnp.concatenate(parts, axis=0).astype(BF16)
        a = jnp.dot(gg_ref[g], uv, preferred_element_type=F32)
        a_ref[g * jb:(g + 1) * jb] = a.reshape(jb, 2 * n1, D_BRANCH).astype(BF16)


def _fnet2_kernel(ar_ref, ai_ref, gate_ref, cs_ref, o_ref, *, norm):
    ar = jnp.swapaxes(ar_ref[...].astype(F32), 0, 1)
    ai = jnp.swapaxes(ai_ref[...].astype(F32), 0, 1)
    ys = []
    for j in range(DFT_T):
        a = jnp.concatenate([ar[j], ai[j]], axis=0).astype(BF16)
        ys.append(jnp.dot(cs_ref[...], a, preferred_element_type=F32))
    y = jnp.swapaxes(jnp.stack(ys, axis=0), 0, 1)
    o_ref[...] = ((y * norm) * _silu(gate_ref[...].astype(F32))).astype(BF16)


def _fnet(z, mc, ms):
    nz, B, L, _ = z.shape
    n1, n2 = _dft_split(L)
    T = DFT_T
    assert L == n1 * n2 and n1 % T == 0 and DFT_K % (2 * n1) == 0, L
    gg, cs2 = _fnet_tables(L)
    gg, cs2 = jnp.asarray(gg).astype(BF16), jnp.asarray(cs2).astype(BF16)
    full = pl.BlockSpec((D_BRANCH, D_BRANCH), lambda b, i: (0, 0))
    a = pl.pallas_call(
        functools.partial(_fnet1_kernel, n1=n1),
        grid=(B, n2 // T),
        in_specs=[pl.BlockSpec((None, None, n1, T, D_BRANCH), lambda b, i: (ZC_IN, b, 0, i, 0)),
                  pl.BlockSpec((T // _dft_group(n1), DFT_K, DFT_K), lambda b, i: (i, 0, 0)),
                  full, full],
        out_specs=pl.BlockSpec((None, T, 2 * n1, D_BRANCH), lambda b, i: (b, i, 0, 0)),
        out_shape=jax.ShapeDtypeStruct((B, n2, 2 * n1, D_BRANCH), BF16),
        compiler_params=_params(2), name="fnet_stage1",
    )(z.reshape(nz, B, n1, n2, D_BRANCH), gg, mc, ms)

    nk = n1 // T
    y = pl.pallas_call(
        functools.partial(_fnet2_kernel, norm=float(1.0 / np.sqrt(L * D_GROUP))),
        grid=(B, nk),
        in_specs=[pl.BlockSpec((None, n2, T, D_BRANCH), lambda b, i: (b, 0, i, 0)),
                  pl.BlockSpec((None, n2, T, D_BRANCH), lambda b, i: (b, 0, nk + i, 0)),
                  pl.BlockSpec((None, None, n2, T, D_BRANCH), lambda b, i: (ZC_GATE, b, 0, i, 0)),
                  pl.BlockSpec((n2, 2 * n2), lambda b, i: (0, 0))],
        out_specs=pl.BlockSpec((None, n2, T, D_BRANCH), lambda b, i: (b, 0, i, 0)),
        out_shape=jax.ShapeDtypeStruct((B, n2, n1, D_BRANCH), BF16),
        compiler_params=_params(2), name="fnet_stage2",
    )(a, a, z.reshape(nz, B, n2, n1, D_BRANCH), cs2)
    return y.reshape(B, L, D_BRANCH)


def _na_key_col0(cb):
    return int(np.clip(cb * NA_CQ - NA_KW // 2, 0, GRID_W - NA_CK))


@functools.lru_cache(maxsize=None)
def _na_tables():
    i = np.arange(NA_TR)[:, None]
    j = np.arange(NA_KROWS)[None, :]
    row_sel = np.zeros((3, NA_TR, NA_KROWS, 2 * NA_KH - 1), np.float32)
    row_ok = np.zeros((3, NA_TR, NA_KROWS), bool)
    for var, (base, rel) in enumerate(((0, np.zeros(NA_TR, int)), (-NA_TR, np.arange(NA_TR)),
                                       (-NA_KH, np.full(NA_TR, NA_KROWS - NA_KH)))):
        ok = (j >= rel[:, None]) & (j < rel[:, None] + NA_KH)
        dr = base + j - i + NA_KH - 1
        for ii, jj in zip(*np.nonzero(ok)):
            row_sel[var, ii, jj, dr[ii, jj]] = 1.0
        row_ok[var] = ok
    col_sel = np.zeros((NA_NCB, NA_CQ, NA_CK, 2 * NA_KW - 1), np.float32)
    col_ok = np.zeros((NA_NCB, NA_CQ, NA_CK), bool)
    for cb in range(NA_NCB):
        for cq in range(NA_CQ):
            c = cb * NA_CQ + cq
            cst = int(np.clip(c - NA_KW // 2, 0, GRID_W - NA_KW))
            for kl in range(NA_CK):
                kc = _na_key_col0(cb) + kl
                if cst <= kc < cst + NA_KW:
                    col_sel[cb, cq, kl, kc - c + NA_KW - 1] = 1.0
                    col_ok[cb, cq, kl] = True
    ok = row_ok[:, None, :, None, :, None] & col_ok[None, :, None, :, None, :]
    mask = np.where(ok, 0.0, NEG_BIG).astype(np.float32).reshape(3, NA_NCB, 1, NA_BQ, NA_BK)
    return row_sel, col_sel, mask


def _na_bias(na_rpb):
    row_sel, col_sel, mask = _na_tables()
    hi = lax.Precision.HIGHEST
    b = jnp.einsum('vija,lhab->lvhijb', row_sel, na_rpb, precision=hi)
    b = jnp.einsum('lvhijb,cqkb->lvchiqjk', b, col_sel, precision=hi)
    b = b.reshape(DEPTH, 3, NA_NCB, N_GROUPS, NA_BQ, NA_BK) + mask
    return b.reshape(DEPTH, 3, NA_NCB, N_GROUPS * NA_BQ, NA_BK)


def _na_kernel(q_ref, k0_ref, k1_ref, k2_ref, v0_ref, v1_ref, v2_ref, gate_ref, bias_ref, o_ref,
               acc_ref):
    head = lax.broadcasted_iota(jnp.int32, (NA_BQ, D_BRANCH), 1) // D_GROUP
    k = jnp.concatenate([k0_ref[...], k1_ref[...], k2_ref[...]], axis=0).astype(F32)
    v = jnp.concatenate([v0_ref[...], v1_ref[...], v2_ref[...]], axis=0).astype(F32)
    for cb in range(NA_NCB):
        c0 = cb * NA_CQ
        kc0 = _na_key_col0(cb)
        q = jnp.concatenate([q_ref[i * GRID_W + c0:i * GRID_W + c0 + NA_CQ, :]
                             for i in range(NA_TR)], axis=0)
        q = q * jnp.asarray(D_GROUP ** -0.5, BF16)
        zero = jnp.zeros_like(q)
        qs = jnp.concatenate([jnp.where(head == h, q, zero) for h in range(N_GROUPS)], axis=0)
        kb = jnp.concatenate([k[j * GRID_W + kc0:j * GRID_W + kc0 + NA_CK]
                              for j in range(NA_KROWS)], axis=0).astype(BF16)
        vb = jnp.concatenate([v[j * GRID_W + kc0:j * GRID_W + kc0 + NA_CK]
                              for j in range(NA_KROWS)], axis=0).astype(BF16)
        s = lax.dot_general(qs, kb, (((1,), (1,)), ((), ())), preferred_element_type=F32)
        s = s + bias_ref[cb]
        m = jnp.max(s, axis=-1, keepdims=True)
        p = jnp.exp(s - m)
        denom = jnp.sum(p, axis=-1, keepdims=True)
        o = jnp.dot(p.astype(BF16), vb, preferred_element_type=F32) / denom
        out = o[0:NA_BQ]
        for h in range(1, N_GROUPS):
            out = jnp.where(head == h, o[h * NA_BQ:(h + 1) * NA_BQ], out)
        for i in range(NA_TR):
            acc_ref[i * GRID_W + c0:i * GRID_W + c0 + NA_CQ, :] = out[i * NA_CQ:(i + 1) * NA_CQ]
    o_ref[...] = (acc_ref[...] * _silu(gate_ref[...].astype(F32))).astype(BF16)


def _na(z, bias):
    _, B, L, _ = z.shape
    nt = L // NA_TQ
    nkb = NA_KROWS // NA_TR

    def kv(s, d):
        return pl.BlockSpec((None, None, NA_TQ, D_BRANCH),
                            lambda b, i: (s, b, jnp.clip(i - 1, 0, nt - nkb) + d, 0))

    return pl.pallas_call(
        _na_kernel,
        grid=(B, nt),
        in_specs=[_slab(ZD_Q, NA_TQ),
                  kv(ZD_K, 0), kv(ZD_K, 1), kv(ZD_K, 2), kv(ZD_V, 0), kv(ZD_V, 1), kv(ZD_V, 2),
                  _slab(ZD_GATE, NA_TQ),
                  pl.BlockSpec((None, NA_NCB, N_GROUPS * NA_BQ, NA_BK),
                               lambda b, i: (_tile_variant(i, nt), 0, 0, 0))],
        out_specs=_rows(NA_TQ, D_BRANCH),
        out_shape=jax.ShapeDtypeStruct((B, L, D_BRANCH), BF16),
        scratch_shapes=[pltpu.VMEM((NA_TQ, D_BRANCH), F32)],
        compiler_params=_params(2), name="na",
    )(z, z, z, z, z, z, z, z, bias)


def _final_kernel(ya_ref, yb_ref, yc_ref, yd_ref, x_ref, gate_ref, w_ref, fg_ref, o_ref):
    y = _mix_out([r[...] for r in (ya_ref, yb_ref, yc_ref, yd_ref)], w_ref)
    o_ref[...] = _rms(x_ref[...] + gate_ref[...] * y) * fg_ref[...]


def _final(ys, x, gate, w_bf, final_g, tm=TM):
    B, L, _ = x.shape
    return pl.pallas_call(
        _final_kernel,
        grid=(B, L // tm),
        in_specs=[_rows(tm, D_BRANCH)] * 4 + [
            _rows(tm, D_MODEL), pl.BlockSpec((None, 1, D_MODEL), lambda b, i: (b, 0, 0)),
            _const((D_MODEL, D_MODEL)), _const((1, D_MODEL))],
        out_specs=_rows(tm, D_MODEL),
        out_shape=jax.ShapeDtypeStruct((B, L, D_MODEL), F32),
        compiler_params=_params(2), name="final",
    )(*ys, x, gate, w_bf, final_g.reshape(1, D_MODEL))


def _trunk(x, mod, prep, norm_g, pool_scale, sgu_norm_g, final_norm_g):
    B = x.shape[0]
    ys, gate = None, None
    for l in range(DEPTH):
        shift, scl, gate_l = (mod[l, :, n * D_MODEL:(n + 1) * D_MODEL].reshape(B, 1, D_MODEL)
                              for n in range(3))
        x, z, ya, yb = _layer(ys, x, gate, prep["w_out"][l - 1], norm_g[l], shift, scl,
                              prep["w_in"][l], sgu_norm_g[l], prep["sgu_w"][l], prep["sgu_b"][l],
                              prep["pool_w"][l], pool_scale[l])
        yc = _fnet(z, prep["fnet_mc"][l], prep["fnet_ms"][l])
        yd = _na(z, prep["na_bias"][l])
        ys, gate = (ya, yb, yc, yd), gate_l
    return _final(ys, x, gate, prep["w_out"][DEPTH - 1], final_norm_g)


def kernel(x_prompt, x_sample, c_prompt, c_sample, norm_g, w_ada, b_ada, w_in, w_out, pool_w, pool_scale, sgu_norm_g, sgu_w, sgu_b, fnet_w, na_rpb, final_norm_g):
    nb_p, nb_s = c_prompt.shape[0], c_sample.shape[0]
    c_all = jnp.concatenate(
        [c_prompt, c_sample, jnp.zeros((C_PAD - nb_p - nb_s, D_MODEL), F32)], axis=0)
    mod = _adaln(c_all, w_ada, b_ada)

    mc, ms = _fnet_fold(_block_diag(fnet_w))
    prep = {
        "w_in": w_in.astype(BF16),
        "w_out": w_out.astype(BF16),
        "pool_w": _block_diag(pool_w).astype(BF16),
        "sgu_w": sgu_w.astype(BF16),
        "sgu_b": jnp.repeat(jnp.swapaxes(sgu_b, 1, 2), D_GROUP, axis=2),
        "fnet_mc": mc,
        "fnet_ms": ms,
        "na_bias": _na_bias(na_rpb),
    }
    y_prompt = _trunk(x_prompt, mod[:, :nb_p], prep, norm_g, pool_scale, sgu_norm_g, final_norm_g)
    y_sample = _trunk(x_sample, mod[:, nb_p:nb_p + nb_s], prep, norm_g, pool_scale, sgu_norm_g,
                      final_norm_g)
    return (y_prompt, y_sample)
```

```python
import functools

import numpy as np
import jax
import jax.numpy as jnp
from jax import lax
from jax.experimental import pallas as pl
from jax.experimental.pallas import tpu as pltpu

F32 = jnp.float32
BF16 = jnp.bfloat16
BF16_ROWS = 16

D_MODEL = 1024
DEPTH = 4
D_BRANCH = 256
N_GROUPS = 4
D_GROUP = 64
POOL_WINDOWS = (2, 4, 8, 16)
POOL_HALO = BF16_ROWS
CHUNK = 128
GRID_W = 64
NA_KH = 8
NA_KW = 16
N_IN_SLICES = 11
D_IN = N_IN_SLICES * D_BRANCH
RMS_EPS = 1e-6
LN_EPS = 1e-5
NEG_BIG = -1e30

A_IN, A_GATE, B_U, B_V, B_GATE, C_IN, C_GATE, D_Q, D_K, D_V, D_GATE = range(N_IN_SLICES)
Z_SLICES = (C_IN, C_GATE, D_Q, D_K, D_V, D_GATE)
ZC_IN, ZC_GATE, ZD_Q, ZD_K, ZD_V, ZD_GATE = range(len(Z_SLICES))

NA_TR = 4
NA_KROWS = NA_TR + NA_KH
NA_TQ = NA_TR * GRID_W
NA_SUB = 4
NA_CQ = 16
NA_NCB = GRID_W // NA_CQ
NA_CK = 2 * NA_KW
NA_BQ = NA_TR * NA_CQ
NA_BK = NA_KROWS * NA_CK
NA_JG = 128 // NA_CK
C_PAD = 16
DFT_T = BF16_ROWS
DFT_K = 256
DFT_TOKENS = 2048
TM = 512
LAYER_ROWS = 256

VMEM_LIMIT = 56 * 1024 * 1024


def _params(n_axes, vmem=VMEM_LIMIT):
    return pltpu.CompilerParams(dimension_semantics=("arbitrary",) * n_axes,
                                vmem_limit_bytes=vmem)


def _silu(x):
    return x / (1.0 + jnp.exp(-x))


def _gelu(x):
    return x * (0.5 * (1.0 + jnp.tanh(np.sqrt(2.0 / np.pi) * (x + 0.044715 * (x * x * x)))))


def _rms(x):
    return x * lax.rsqrt(jnp.mean(x * x, axis=-1, keepdims=True) + RMS_EPS)


def _tile_variant(i, n):
    return jnp.where(i == 0, 0, jnp.where(i == n - 1, 2, 1))


def _const(shape):
    zeros = (0,) * len(shape)
    return pl.BlockSpec(shape, lambda b, i: zeros)


def _rows(t, width):
    return pl.BlockSpec((None, t, width), lambda b, i: (b, i, 0))


def _slab(s, t):
    return pl.BlockSpec((None, None, t, D_BRANCH), lambda b, i: (s, b, i, 0))


def _adaln_kernel(c_ref, w_ref, b_ref, o_ref):
    s = _silu(c_ref[...])
    o_ref[...] = jnp.dot(s, w_ref[...], precision=lax.Precision.HIGHEST,
                         preferred_element_type=F32) + b_ref[...]


def _adaln(c_all, w_ada, b_ada):
    return pl.pallas_call(
        _adaln_kernel,
        grid=(DEPTH, 3),
        in_specs=[pl.BlockSpec((C_PAD, D_MODEL), lambda l, n: (0, 0)),
                  pl.BlockSpec((None, D_MODEL, D_MODEL), lambda l, n: (l, 0, n)),
                  pl.BlockSpec((None, 1, D_MODEL), lambda l, n: (l, 0, n))],
        out_specs=pl.BlockSpec((None, C_PAD, D_MODEL), lambda l, n: (l, 0, n)),
        out_shape=jax.ShapeDtypeStruct((DEPTH, C_PAD, 3 * D_MODEL), F32),
        compiler_params=_params(2), name="adaln",
    )(c_all, w_ada, b_ada.reshape(DEPTH, 1, 3 * D_MODEL))


@functools.lru_cache(maxsize=None)
def _pool_tables(tp):
    r = np.arange(CHUNK)[:, None]
    c = np.arange(CHUNK + 2 * POOL_HALO)[None, :]
    band = np.stack([(c >= r + POOL_HALO - w // 2) & (c < r + POOL_HALO + w // 2)
                     for w in POOL_WINDOWS]).astype(np.float32)
    t = np.arange(tp)[:, None]
    w = np.repeat(np.array(POOL_WINDOWS), D_GROUP)[None, :]
    head = np.minimum(t + w // 2, tp + w) - np.maximum(t - w // 2, 0)
    tail = np.minimum(t + w // 2, tp) - np.maximum(t - w // 2, -w)
    inv = np.stack([1.0 / head, 1.0 / np.broadcast_to(w, head.shape), 1.0 / tail])
    return band, inv.astype(np.float32)


def _mix_out(ys, w_ref):
    y = jnp.dot(ys[0], w_ref[0:D_BRANCH, :], preferred_element_type=F32)
    for n in range(1, len(ys)):
        y = y + jnp.dot(ys[n], w_ref[n * D_BRANCH:(n + 1) * D_BRANCH, :],
                        preferred_element_type=F32)
    return y


def _modulate(x, g_ref, shift_ref, scl_ref):
    return ((_rms(x) * g_ref[...]) * (1.0 + scl_ref[...]) + shift_ref[...]).astype(BF16)


def _project(h, w_ref, s):
    return jnp.dot(h, w_ref[:, s * D_BRANCH:(s + 1) * D_BRANCH], preferred_element_type=F32)


def _sgu_body(u, v, gate, g_ref, w_ref, b_ref, o_ref):
    t = u.shape[0]
    u = _gelu(u)
    v = _gelu(v)
    mu = jnp.mean(v, axis=-1, keepdims=True)
    d = v - mu
    var = jnp.mean(d * d, axis=-1, keepdims=True)
    vn = (d * lax.rsqrt(var + LN_EPS) * g_ref[...]).astype(BF16)
    gate = _silu(gate)
    head = lax.broadcasted_iota(jnp.int32, (CHUNK, D_BRANCH), 1) // D_GROUP
    for n in range(t // CHUNK):
        rows = slice(n * CHUNK, (n + 1) * CHUNK)
        vc = vn[rows]
        s = jnp.dot(w_ref[0], vc, preferred_element_type=F32)
        for h in range(1, N_GROUPS):
            s = jnp.where(head == h, jnp.dot(w_ref[h], vc, preferred_element_type=F32), s)
        o_ref[rows, :] = ((u[rows] * (s + b_ref[...])) * gate[rows]).astype(BF16)


def _pool_body(ext_ref, gate, band_ref, inv_ref, w_ref, scale_ref, o_ref):
    t = gate.shape[0]
    H = POOL_HALO
    gate = _silu(gate)
    group = lax.broadcasted_iota(jnp.int32, (CHUNK, D_BRANCH), 1) // D_GROUP
    for m in range(t // CHUNK):
        rows = slice(m * CHUNK, (m + 1) * CHUNK)
        e = ext_ref[m * CHUNK:(m + 1) * CHUNK + 2 * H, :]
        s = jnp.dot(band_ref[0], e, preferred_element_type=F32)
        for g in range(1, N_GROUPS):
            s = jnp.where(group == g, jnp.dot(band_ref[g], e, preferred_element_type=F32), s)
        p = s * inv_ref[rows, :] - e[H:H + CHUNK].astype(F32)
        y = jnp.dot(p.astype(BF16), w_ref[...], preferred_element_type=F32) * scale_ref[...]
        o_ref[rows, :] = (y * gate[rows]).astype(BF16)


def _layer_kernel(*refs, tm, first):
    n_stream = 2 if first else 12
    stream, rest = refs[:n_stream], refs[n_stream:]
    (g_ref, shift_ref, scl_ref, win_ref, sg_ref, sw_ref, sb_ref,
     band_ref, inv_ref, pw_ref, ps_ref) = rest[:11]
    outs = rest[11:]
    i = pl.program_id(1)
    n = pl.num_programs(1)
    H = POOL_HALO
    ext_ref, carry_ref = outs[-2:]
    slot_prev = (i + 1) % 2
    slot_this = i % 2

    @pl.when(i == 0)
    def _():
        carry_ref[slot_prev] = jnp.zeros((H, D_BRANCH), BF16)

    if first:
        x_ref, xh_ref = stream
        z_ref, ya_ref, yb_ref = outs[:3]
        h_ext = _modulate(jnp.concatenate([x_ref[...], xh_ref[...]], axis=0),
                          g_ref, shift_ref, scl_ref)
    else:
        ys, x_ref, ysh, xh_ref, gprev_ref, wout_ref = (stream[0:4], stream[4], stream[5:9],
                                                       stream[9], stream[10], stream[11])
        xo_ref, z_ref, ya_ref, yb_ref = outs[:4]
        ys_ext = [jnp.concatenate([m[...], mh[...]], axis=0) for m, mh in zip(ys, ysh)]
        h_parts = []
        for r0 in range(0, tm, LAYER_ROWS):
            r1 = r0 + LAYER_ROWS
            if r1 == tm:
                r1 = tm + H
                x_in = jnp.concatenate([x_ref[r0:tm, :], xh_ref[...]], axis=0)
            else:
                x_in = x_ref[r0:r1, :]
            x = x_in + gprev_ref[...] * _mix_out([m[r0:r1] for m in ys_ext], wout_ref)
            xo_ref[r0:min(r1, tm), :] = x[0:min(r1, tm) - r0]
            h_parts.append(_modulate(x, g_ref, shift_ref, scl_ref))
        h_ext = jnp.concatenate(h_parts, axis=0)

    h = h_ext[0:tm]
    zl = jnp.dot(h_ext, win_ref[:, 0:Z_SLICES[0] * D_BRANCH], preferred_element_type=F32)

    def local(s):
        return zl[0:tm, s * D_BRANCH:(s + 1) * D_BRANCH]

    _sgu_body(local(B_U), local(B_V), local(B_GATE), sg_ref, sw_ref, sb_ref, yb_ref)

    a_ext = zl[:, A_IN * D_BRANCH:(A_IN + 1) * D_BRANCH].astype(BF16)
    a = a_ext[0:tm]
    a_next = a_ext[tm:tm + H]
    ext_ref[0:H, :] = carry_ref[slot_prev]
    ext_ref[H:H + tm, :] = a
    ext_ref[H + tm:2 * H + tm, :] = jnp.where(i < n - 1, a_next, jnp.zeros_like(a_next))
    carry_ref[slot_this] = a[tm - H:tm]
    _pool_body(ext_ref, local(A_GATE), band_ref, inv_ref, pw_ref, ps_ref, ya_ref)

    zg = jnp.dot(h, win_ref[:, Z_SLICES[0] * D_BRANCH:], preferred_element_type=F32)
    for k in range(len(Z_SLICES)):
        z_ref[k] = zg[:, k * D_BRANCH:(k + 1) * D_BRANCH].astype(BF16)


def _layer(ys, x, gate_prev, w_out, norm_g, shift, scl, w_in, sgu_g, sgu_w, sgu_b, pool_w,
           pool_scale, tm=TM):
    B, L, _ = x.shape
    first = ys is None
    H = POOL_HALO
    nt = L // tm
    nh = L // H
    band, inv = _pool_tables(tm)

    def halo(width):
        return pl.BlockSpec((None, H, width),
                            lambda b, i: (b, jnp.minimum((i + 1) * (tm // H), nh - 1), 0))

    vec = pl.BlockSpec((None, 1, D_MODEL), lambda b, i: (b, 0, 0))
    if first:
        stream_specs = [_rows(tm, D_MODEL), halo(D_MODEL)]
        stream_args = [x, x]
    else:
        stream_specs = ([_rows(tm, D_BRANCH)] * 4 + [_rows(tm, D_MODEL)] + [halo(D_BRANCH)] * 4
                        + [halo(D_MODEL), vec, _const((D_MODEL, D_MODEL))])
        stream_args = [*ys, x, *ys, x, gate_prev, w_out]
    param_specs = [_const((1, D_MODEL)), vec, vec, _const((D_MODEL, D_IN)),
                   _const((1, D_BRANCH)), _const((N_GROUPS, CHUNK, CHUNK)), _const((CHUNK, D_BRANCH)),
                   _const(band.shape),
                   pl.BlockSpec((None, tm, D_BRANCH), lambda b, i: (_tile_variant(i, nt), 0, 0)),
                   _const((D_BRANCH, D_BRANCH)), _const((1, D_BRANCH))]
    param_args = [norm_g.reshape(1, D_MODEL), shift, scl, w_in,
                  sgu_g.reshape(1, D_BRANCH), sgu_w, sgu_b,
                  jnp.asarray(band).astype(BF16), jnp.asarray(inv), pool_w,
                  pool_scale.reshape(1, D_BRANCH)]
    nz = len(Z_SLICES)
    out_specs = [pl.BlockSpec((nz, None, tm, D_BRANCH), lambda b, i: (0, b, i, 0)),
                 _rows(tm, D_BRANCH), _rows(tm, D_BRANCH)]
    out_shape = [jax.ShapeDtypeStruct((nz, B, L, D_BRANCH), BF16),
                 jax.ShapeDtypeStruct((B, L, D_BRANCH), BF16),
                 jax.ShapeDtypeStruct((B, L, D_BRANCH), BF16)]
    if not first:
        out_specs = [_rows(tm, D_MODEL)] + out_specs
        out_shape = [jax.ShapeDtypeStruct((B, L, D_MODEL), F32)] + out_shape
    res = pl.pallas_call(
        functools.partial(_layer_kernel, tm=tm, first=first),
        grid=(B, nt),
        in_specs=stream_specs + param_specs,
        out_specs=out_specs,
        out_shape=out_shape,
        scratch_shapes=[pltpu.VMEM((tm + 2 * H, D_BRANCH), BF16),
                        pltpu.VMEM((2, H, D_BRANCH), BF16)],
        compiler_params=_params(2), name="layer_first" if first else "layer",
    )(*stream_args, *param_args)
    return (x, *res) if first else tuple(res)


def _dft_split(L):
    n2 = 128
    return L // n2, n2


def _dft_group(n1):
    return DFT_K // (2 * n1)


@functools.lru_cache(maxsize=None)
def _fnet_tables(L):
    n1, n2 = _dft_split(L)
    k1 = np.arange(n1, dtype=np.int64)
    n = np.arange(n2, dtype=np.int64)[:, None, None] + n2 * np.arange(n1, dtype=np.int64)[None, None, :]
    ang = 2.0 * np.pi * ((k1[None, :, None] * n) % L).astype(np.float64) / L
    gc, gs = np.cos(ang), np.sin(ang)
    gg = np.concatenate([np.concatenate([gc, -gs], axis=2),
                         np.concatenate([-gs, -gc], axis=2)], axis=1)
    jb = _dft_group(n1)
    grouped = np.zeros((n2 // jb, jb, 2 * n1, jb, 2 * n1))
    for j in range(jb):
        grouped[:, j, :, j, :] = gg[j::jb]
    gg = grouped.reshape(n2 // jb, DFT_K, DFT_K)
    kk = np.arange(n2, dtype=np.int64)
    ang2 = 2.0 * np.pi * ((kk[:, None] * kk[None, :]) % n2).astype(np.float64) / n2
    cs2 = np.concatenate([np.cos(ang2), np.sin(ang2)], axis=1)
    return gg.astype(np.float32), cs2.astype(np.float32)


@functools.lru_cache(maxsize=None)
def _channel_tables():
    c = np.arange(D_GROUP, dtype=np.int64)
    ang = 2.0 * np.pi * ((c[:, None] * c[None, :]) % D_GROUP).astype(np.float64) / D_GROUP
    eye = np.eye(N_GROUPS)
    return (np.kron(eye, np.cos(ang)).astype(np.float32),
            np.kron(eye, np.sin(ang)).astype(np.float32))


def _block_diag(w):
    eye = jnp.eye(N_GROUPS, dtype=w.dtype)
    out = jnp.einsum('...gcd,gh->...gchd', w, eye)
    return out.reshape(*w.shape[:-3], D_BRANCH, D_BRANCH)


def _fold_kernel(bdc_ref, bds_ref, w_ref, mc_ref, ms_ref):
    w = w_ref[...]
    hi = lax.Precision.HIGHEST
    mc_ref[...] = jnp.dot(bdc_ref[...], w, precision=hi, preferred_element_type=F32).astype(BF16)
    ms_ref[...] = jnp.dot(bds_ref[...], w, precision=hi, preferred_element_type=F32).astype(BF16)


def _fnet_fold(fnet_w_bd):
    bdc, bds = _channel_tables()
    full = pl.BlockSpec((D_BRANCH, D_BRANCH), lambda l: (0, 0))
    per = pl.BlockSpec((None, D_BRANCH, D_BRANCH), lambda l: (l, 0, 0))
    shp = jax.ShapeDtypeStruct((DEPTH, D_BRANCH, D_BRANCH), BF16)
    return pl.pallas_call(
        _fold_kernel, grid=(DEPTH,), in_specs=[full, full, per], out_specs=[per, per],
        out_shape=[shp, shp], compiler_params=_params(1), name="fnet_fold",
    )(jnp.asarray(bdc), jnp.asarray(bds), fnet_w_bd)


def _fnet1_kernel(x_ref, gg_ref, mc_ref, ms_ref, a_ref, *, n1, t):
    jb = _dft_group(n1)
    xt = jnp.swapaxes(x_ref[...].astype(F32), 0, 1)
    x2 = xt.reshape(t * n1, D_BRANCH).astype(BF16)
    u = jnp.dot(x2, mc_ref[...], preferred_element_type=F32)
    v = jnp.dot(x2, ms_ref[...], preferred_element_type=F32)
    for g in range(t // jb):
        parts = []
        for j in range(g * jb, (g + 1) * jb):
            parts += [u[j * n1:(j + 1) * n1], v[j * n1:(j + 1) * n1]]
        uv = jnp.concatenate(parts, axis=0).astype(BF16)
        a = jnp.dot(gg_ref[g], uv, preferred_element_type=F32)
        a_ref[g * jb:(g + 1) * jb] = a.reshape(jb, 2 * n1, D_BRANCH).astype(BF16)


def _fnet2_kernel(ar_ref, ai_ref, gate_ref, cs_ref, o_ref, *, norm):
    ar = jnp.swapaxes(ar_ref[...].astype(F32), 0, 1)
    ai = jnp.swapaxes(ai_ref[...].astype(F32), 0, 1)
    ys = []
    for j in range(DFT_T):
        a = jnp.concatenate([ar[j], ai[j]], axis=0).astype(BF16)
        ys.append(jnp.dot(cs_ref[...], a, preferred_element_type=F32))
    y = jnp.swapaxes(jnp.stack(ys, axis=0), 0, 1)
    o_ref[...] = ((y * norm) * _silu(gate_ref[...].astype(F32))).astype(BF16)


def _fnet(z, mc, ms):
    nz, B, L, _ = z.shape
    n1, n2 = _dft_split(L)
    T = DFT_T
    assert L == n1 * n2 and n1 % T == 0 and DFT_K % (2 * n1) == 0, L
    gg, cs2 = _fnet_tables(L)
    gg, cs2 = jnp.asarray(gg).astype(BF16), jnp.asarray(cs2).astype(BF16)
    full = pl.BlockSpec((D_BRANCH, D_BRANCH), lambda b, i: (0, 0))
    t1 = min(n2, max(T, DFT_TOKENS // n1))
    a = pl.pallas_call(
        functools.partial(_fnet1_kernel, n1=n1, t=t1),
        grid=(B, n2 // t1),
        in_specs=[pl.BlockSpec((None, None, n1, t1, D_BRANCH), lambda b, i: (ZC_IN, b, 0, i, 0)),
                  pl.BlockSpec((t1 // _dft_group(n1), DFT_K, DFT_K), lambda b, i: (i, 0, 0)),
                  full, full],
        out_specs=pl.BlockSpec((None, t1, 2 * n1, D_BRANCH), lambda b, i: (b, i, 0, 0)),
        out_shape=jax.ShapeDtypeStruct((B, n2, 2 * n1, D_BRANCH), BF16),
        compiler_params=_params(2), name="fnet_stage1",
    )(z.reshape(nz, B, n1, n2, D_BRANCH), gg, mc, ms)

    nk = n1 // T
    y = pl.pallas_call(
        functools.partial(_fnet2_kernel, norm=float(1.0 / np.sqrt(L * D_GROUP))),
        grid=(B, nk),
        in_specs=[pl.BlockSpec((None, n2, T, D_BRANCH), lambda b, i: (b, 0, i, 0)),
                  pl.BlockSpec((None, n2, T, D_BRANCH), lambda b, i: (b, 0, nk + i, 0)),
                  pl.BlockSpec((None, None, n2, T, D_BRANCH), lambda b, i: (ZC_GATE, b, 0, i, 0)),
                  pl.BlockSpec((n2, 2 * n2), lambda b, i: (0, 0))],
        out_specs=pl.BlockSpec((None, n2, T, D_BRANCH), lambda b, i: (b, 0, i, 0)),
        out_shape=jax.ShapeDtypeStruct((B, n2, n1, D_BRANCH), BF16),
        compiler_params=_params(2), name="fnet_stage2",
    )(a, a, z.reshape(nz, B, n2, n1, D_BRANCH), cs2)
    return y.reshape(B, L, D_BRANCH)


def _na_key_col0(cb):
    return int(np.clip(cb * NA_CQ - NA_KW // 2, 0, GRID_W - NA_CK))


@functools.lru_cache(maxsize=None)
def _na_tables():
    i = np.arange(NA_TR)[:, None]
    j = np.arange(NA_KROWS)[None, :]
    row_sel = np.zeros((3, NA_TR, NA_KROWS, 2 * NA_KH - 1), np.float32)
    row_ok = np.zeros((3, NA_TR, NA_KROWS), bool)
    for var, (base, rel) in enumerate(((0, np.zeros(NA_TR, int)), (-NA_TR, np.arange(NA_TR)),
                                       (-NA_KH, np.full(NA_TR, NA_KROWS - NA_KH)))):
        ok = (j >= rel[:, None]) & (j < rel[:, None] + NA_KH)
        dr = base + j - i + NA_KH - 1
        for ii, jj in zip(*np.nonzero(ok)):
            row_sel[var, ii, jj, dr[ii, jj]] = 1.0
        row_ok[var] = ok
    col_sel = np.zeros((NA_NCB, NA_CQ, NA_CK, 2 * NA_KW - 1), np.float32)
    col_ok = np.zeros((NA_NCB, NA_CQ, NA_CK), bool)
    for cb in range(NA_NCB):
        for cq in range(NA_CQ):
            c = cb * NA_CQ + cq
            cst = int(np.clip(c - NA_KW // 2, 0, GRID_W - NA_KW))
            for kl in range(NA_CK):
                kc = _na_key_col0(cb) + kl
                if cst <= kc < cst + NA_KW:
                    col_sel[cb, cq, kl, kc - c + NA_KW - 1] = 1.0
                    col_ok[cb, cq, kl] = True
    ok = row_ok[:, None, :, None, :, None] & col_ok[None, :, None, :, None, :]
    mask = np.where(ok, 0.0, NEG_BIG).astype(np.float32).reshape(3, NA_NCB, 1, NA_BQ, NA_BK)
    nb = 2 * NA_KW - 1
    col_sel_g = np.zeros((NA_JG, nb, NA_NCB, NA_CQ, NA_JG, NA_CK), np.float32)
    for j in range(NA_JG):
        col_sel_g[j, :, :, :, j, :] = np.transpose(col_sel, (3, 0, 1, 2))
    return row_sel, col_sel_g.reshape(NA_JG * nb, NA_NCB * NA_CQ * NA_JG * NA_CK), mask


def _na_bias(na_rpb):
    row_sel, col_sel_g, mask = _na_tables()
    hi = lax.Precision.HIGHEST
    ng = NA_KROWS // NA_JG
    b = jnp.einsum('vija,lhab->lvhijb', row_sel, na_rpb, precision=hi)
    b = b.reshape(DEPTH * 3 * N_GROUPS * NA_TR, ng, col_sel_g.shape[0])
    b = jnp.einsum('rgx,xn->rgn', b, col_sel_g, precision=hi)
    b = b.reshape(DEPTH, 3, N_GROUPS, NA_TR, ng, NA_NCB, NA_CQ, NA_JG * NA_CK)
    b = jnp.transpose(b, (0, 1, 5, 2, 3, 6, 4, 7))
    b = b.reshape(DEPTH, 3, NA_NCB, N_GROUPS, NA_BQ, NA_BK) + mask
    return b.reshape(DEPTH, 3, NA_NCB, N_GROUPS * NA_BQ, NA_BK)


def _na_kernel(*refs):
    q_ref, gate_ref = refs[:2]
    o_ref, acc_ref = refs[-2:]
    for t in range(NA_SUB):
        tile = refs[2 + 7 * t:9 + 7 * t]
        _na_tile(q_ref, t * NA_TQ, tile[0:3], tile[3:6], tile[6], acc_ref)
    o_ref[...] = (acc_ref[...] * _silu(gate_ref[...].astype(F32))).astype(BF16)


def _na_tile(q_ref, row0, k_refs, v_refs, bias_ref, acc_ref):
    head = lax.broadcasted_iota(jnp.int32, (NA_BQ, D_BRANCH), 1) // D_GROUP
    k = jnp.concatenate([r[...] for r in k_refs], axis=0).astype(F32)
    v = jnp.concatenate([r[...] for r in v_refs], axis=0).astype(F32)
    for cb in range(NA_NCB):
        c0 = row0 + cb * NA_CQ
        kc0 = _na_key_col0(cb)
        q = jnp.concatenate([q_ref[i * GRID_W + c0:i * GRID_W + c0 + NA_CQ, :]
                             for i in range(NA_TR)], axis=0)
        q = q * jnp.asarray(D_GROUP ** -0.5, BF16)
        zero = jnp.zeros_like(q)
        qs = jnp.concatenate([jnp.where(head == h, q, zero) for h in range(N_GROUPS)], axis=0)
        kb = jnp.concatenate([k[j * GRID_W + kc0:j * GRID_W + kc0 + NA_CK]
                              for j in range(NA_KROWS)], axis=0).astype(BF16)
        vb = jnp.concatenate([v[j * GRID_W + kc0:j * GRID_W + kc0 + NA_CK]
                              for j in range(NA_KROWS)], axis=0).astype(BF16)
        s = lax.dot_general(qs, kb, (((1,), (1,)), ((), ())), preferred_element_type=F32)
        s = s + bias_ref[cb]
        m = jnp.max(s, axis=-1, keepdims=True)
        p = jnp.exp(s - m)
        denom = jnp.sum(p, axis=-1, keepdims=True)
        o = jnp.dot(p.astype(BF16), vb, preferred_element_type=F32) / denom
        out = o[0:NA_BQ]
        for h in range(1, N_GROUPS):
            out = jnp.where(head == h, o[h * NA_BQ:(h + 1) * NA_BQ], out)
        for i in range(NA_TR):
            acc_ref[i * GRID_W + c0:i * GRID_W + c0 + NA_CQ, :] = out[i * NA_CQ:(i + 1) * NA_CQ]


def _na(z, bias):
    _, B, L, _ = z.shape
    nt = L // NA_TQ
    nkb = NA_KROWS // NA_TR
    assert nt % NA_SUB == 0 and nt >= nkb, L

    def kv(s, t, d):
        return pl.BlockSpec(
            (None, None, NA_TQ, D_BRANCH),
            lambda b, i: (s, b, jnp.clip(i * NA_SUB + t - 1, 0, nt - nkb) + d, 0))

    tile_specs, tile_args = [], []
    for t in range(NA_SUB):
        tile_specs += [kv(ZD_K, t, d) for d in range(nkb)] + [kv(ZD_V, t, d) for d in range(nkb)]
        tile_specs.append(pl.BlockSpec(
            (None, NA_NCB, N_GROUPS * NA_BQ, NA_BK),
            lambda b, i, t=t: (_tile_variant(i * NA_SUB + t, nt), 0, 0, 0)))
        tile_args += [z] * (2 * nkb) + [bias]
    return pl.pallas_call(
        _na_kernel,
        grid=(B, nt // NA_SUB),
        in_specs=[_slab(ZD_Q, NA_SUB * NA_TQ), _slab(ZD_GATE, NA_SUB * NA_TQ)] + tile_specs,
        out_specs=_rows(NA_SUB * NA_TQ, D_BRANCH),
        out_shape=jax.ShapeDtypeStruct((B, L, D_BRANCH), BF16),
        scratch_shapes=[pltpu.VMEM((NA_SUB * NA_TQ, D_BRANCH), F32)],
        compiler_params=_params(2), name="na",
    )(z, z, *tile_args)


def _final_kernel(ya_ref, yb_ref, yc_ref, yd_ref, x_ref, gate_ref, w_ref, fg_ref, o_ref):
    y = _mix_out([r[...] for r in (ya_ref, yb_ref, yc_ref, yd_ref)], w_ref)
    o_ref[...] = _rms(x_ref[...] + gate_ref[...] * y) * fg_ref[...]


def _final(ys, x, gate, w_bf, final_g, tm=TM):
    B, L, _ = x.shape
    return pl.pallas_call(
        _final_kernel,
        grid=(B, L // tm),
        in_specs=[_rows(tm, D_BRANCH)] * 4 + [
            _rows(tm, D_MODEL), pl.BlockSpec((None, 1, D_MODEL), lambda b, i: (b, 0, 0)),
            _const((D_MODEL, D_MODEL)), _const((1, D_MODEL))],
        out_specs=_rows(tm, D_MODEL),
        out_shape=jax.ShapeDtypeStruct((B, L, D_MODEL), F32),
        compiler_params=_params(2), name="final",
    )(*ys, x, gate, w_bf, final_g.reshape(1, D_MODEL))


def _trunk(x, mod, prep, norm_g, pool_scale, sgu_norm_g, final_norm_g):
    B = x.shape[0]
    ys, gate = None, None
    for l in range(DEPTH):
        shift, scl, gate_l = (mod[l, :, n * D_MODEL:(n + 1) * D_MODEL].reshape(B, 1, D_MODEL)
                              for n in range(3))
        x, z, ya, yb = _layer(ys, x, gate, prep["w_out"][l - 1], norm_g[l], shift, scl,
                              prep["w_in"][l], sgu_norm_g[l], prep["sgu_w"][l], prep["sgu_b"][l],
                              prep["pool_w"][l], pool_scale[l])
        yc = _fnet(z, prep["fnet_mc"][l], prep["fnet_ms"][l])
        yd = _na(z, prep["na_bias"][l])
        ys, gate = (ya, yb, yc, yd), gate_l
    return _final(ys, x, gate, prep["w_out"][DEPTH - 1], final_norm_g)


def kernel(x_prompt, x_sample, c_prompt, c_sample, norm_g, w_ada, b_ada, w_in, w_out, pool_w, pool_scale, sgu_norm_g, sgu_w, sgu_b, fnet_w, na_rpb, final_norm_g):
    nb_p, nb_s = c_prompt.shape[0], c_sample.shape[0]
    c_all = jnp.concatenate(
        [c_prompt, c_sample, jnp.zeros((C_PAD - nb_p - nb_s, D_MODEL), F32)], axis=0)
    mod = _adaln(c_all, w_ada, b_ada)

    mc, ms = _fnet_fold(_block_diag(fnet_w))
    prep = {
        "w_in": w_in.astype(BF16),
        "w_out": w_out.astype(BF16),
        "pool_w": _block_diag(pool_w).astype(BF16),
        "sgu_w": sgu_w.astype(BF16),
        "sgu_b": jnp.repeat(jnp.swapaxes(sgu_b, 1, 2), D_GROUP, axis=2),
        "fnet_mc": mc,
        "fnet_ms": ms,
        "na_bias": _na_bias(na_rpb),
    }
    y_prompt = _trunk(x_prompt, mod[:, :nb_p], prep, norm_g, pool_scale, sgu_norm_g, final_norm_g)
    y_sample = _trunk(x_sample, mod[:, nb_p:nb_p + nb_s], prep, norm_g, pool_scale, sgu_norm_g,
                      final_norm_g)
    return (y_prompt, y_sample)
```

```python
import functools

import numpy as np
import jax
import jax.numpy as jnp
from jax import lax
from jax.experimental import pallas as pl
from jax.experimental.pallas import tpu as pltpu

F32 = jnp.float32
BF16 = jnp.bfloat16
BF16_ROWS = 16

D_MODEL = 1024
DEPTH = 4
D_BRANCH = 256
N_GROUPS = 4
D_GROUP = 64
POOL_WINDOWS = (2, 4, 8, 16)
POOL_HALO = BF16_ROWS
CHUNK = 128
GRID_W = 64
NA_KH = 8
NA_KW = 16
N_IN_SLICES = 11
D_IN = N_IN_SLICES * D_BRANCH
RMS_EPS = 1e-6
LN_EPS = 1e-5
NEG_BIG = -1e30

A_IN, A_GATE, B_U, B_V, B_GATE, C_IN, C_GATE, D_Q, D_K, D_V, D_GATE = range(N_IN_SLICES)
Z_SLICES = (C_IN, C_GATE, D_Q, D_K, D_V, D_GATE)
ZC_IN, ZC_GATE, ZD_Q, ZD_K, ZD_V, ZD_GATE = range(len(Z_SLICES))

NA_TR = 4
NA_KROWS = NA_TR + NA_KH
NA_TQ = NA_TR * GRID_W
NA_SUB = 4
NA_CQ = 16
NA_NCB = GRID_W // NA_CQ
NA_CK = 2 * NA_KW
NA_BQ = NA_TR * NA_CQ
NA_BK = NA_KROWS * NA_CK
NA_JG = 128 // NA_CK
C_PAD = 16
DFT_T = BF16_ROWS
DFT_K = 256
DFT_TOKENS = 2048
TM = 512
LAYER_ROWS = 256

VMEM_LIMIT = 56 * 1024 * 1024


def _params(n_axes, vmem=VMEM_LIMIT):
    return pltpu.CompilerParams(dimension_semantics=("arbitrary",) * n_axes,
                                vmem_limit_bytes=vmem)


def _silu(x):
    return x / (1.0 + jnp.exp(-x))


def _gelu(x):
    return x * (0.5 * (1.0 + jnp.tanh(np.sqrt(2.0 / np.pi) * (x + 0.044715 * (x * x * x)))))


def _rms(x):
    return x * lax.rsqrt(jnp.mean(x * x, axis=-1, keepdims=True) + RMS_EPS)


def _tile_variant(i, n):
    return jnp.where(i == 0, 0, jnp.where(i == n - 1, 2, 1))


def _const(shape):
    zeros = (0,) * len(shape)
    return pl.BlockSpec(shape, lambda b, i: zeros)


def _rows(t, width):
    return pl.BlockSpec((None, t, width), lambda b, i: (b, i, 0))


def _slab(s, t):
    return pl.BlockSpec((None, None, t, D_BRANCH), lambda b, i: (s, b, i, 0))


def _adaln_kernel(c_ref, w_ref, b_ref, o_ref):
    s = _silu(c_ref[...])
    o_ref[...] = jnp.dot(s, w_ref[...], precision=lax.Precision.HIGHEST,
                         preferred_element_type=F32) + b_ref[...]


def _adaln(c_all, w_ada, b_ada):
    return pl.pallas_call(
        _adaln_kernel,
        grid=(DEPTH, 3),
        in_specs=[pl.BlockSpec((C_PAD, D_MODEL), lambda l, n: (0, 0)),
                  pl.BlockSpec((None, D_MODEL, D_MODEL), lambda l, n: (l, 0, n)),
                  pl.BlockSpec((None, 1, D_MODEL), lambda l, n: (l, 0, n))],
        out_specs=pl.BlockSpec((None, C_PAD, D_MODEL), lambda l, n: (l, 0, n)),
        out_shape=jax.ShapeDtypeStruct((DEPTH, C_PAD, 3 * D_MODEL), F32),
        compiler_params=_params(2), name="adaln",
    )(c_all, w_ada, b_ada.reshape(DEPTH, 1, 3 * D_MODEL))


@functools.lru_cache(maxsize=None)
def _pool_tables(tp):
    r = np.arange(CHUNK)[:, None]
    c = np.arange(CHUNK + 2 * POOL_HALO)[None, :]
    band = np.stack([(c >= r + POOL_HALO - w // 2) & (c < r + POOL_HALO + w // 2)
                     for w in POOL_WINDOWS]).astype(np.float32)
    t = np.arange(tp)[:, None]
    w = np.repeat(np.array(POOL_WINDOWS), D_GROUP)[None, :]
    head = np.minimum(t + w // 2, tp + w) - np.maximum(t - w // 2, 0)
    tail = np.minimum(t + w // 2, tp) - np.maximum(t - w // 2, -w)
    inv = np.stack([1.0 / head, 1.0 / np.broadcast_to(w, head.shape), 1.0 / tail])
    return band, inv.astype(np.float32)


def _gated(y, gate):
    return (y.astype(F32) * _silu(gate.astype(F32))).astype(BF16)


def _mix_out(ys, w_ref):
    y = jnp.dot(ys[0], w_ref[0:D_BRANCH, :], preferred_element_type=F32)
    for n in range(1, len(ys)):
        y = y + jnp.dot(ys[n], w_ref[n * D_BRANCH:(n + 1) * D_BRANCH, :],
                        preferred_element_type=F32)
    return y


def _modulate(x, g_ref, shift_ref, scl_ref):
    return ((_rms(x) * g_ref[...]) * (1.0 + scl_ref[...]) + shift_ref[...]).astype(BF16)


def _project(h, w_ref, s):
    return jnp.dot(h, w_ref[:, s * D_BRANCH:(s + 1) * D_BRANCH], preferred_element_type=F32)


def _sgu_body(u, v, gate, g_ref, w_ref, b_ref, o_ref):
    t = u.shape[0]
    u = _gelu(u)
    v = _gelu(v)
    mu = jnp.mean(v, axis=-1, keepdims=True)
    d = v - mu
    var = jnp.mean(d * d, axis=-1, keepdims=True)
    vn = (d * lax.rsqrt(var + LN_EPS) * g_ref[...]).astype(BF16)
    gate = _silu(gate)
    head = lax.broadcasted_iota(jnp.int32, (CHUNK, D_BRANCH), 1) // D_GROUP
    for n in range(t // CHUNK):
        rows = slice(n * CHUNK, (n + 1) * CHUNK)
        vc = vn[rows]
        s = jnp.dot(w_ref[0], vc, preferred_element_type=F32)
        for h in range(1, N_GROUPS):
            s = jnp.where(head == h, jnp.dot(w_ref[h], vc, preferred_element_type=F32), s)
        o_ref[rows, :] = ((u[rows] * (s + b_ref[...])) * gate[rows]).astype(BF16)


def _pool_body(ext_ref, gate, band_ref, inv_ref, w_ref, scale_ref, o_ref):
    t = gate.shape[0]
    H = POOL_HALO
    gate = _silu(gate)
    group = lax.broadcasted_iota(jnp.int32, (CHUNK, D_BRANCH), 1) // D_GROUP
    for m in range(t // CHUNK):
        rows = slice(m * CHUNK, (m + 1) * CHUNK)
        e = ext_ref[m * CHUNK:(m + 1) * CHUNK + 2 * H, :]
        s = jnp.dot(band_ref[0], e, preferred_element_type=F32)
        for g in range(1, N_GROUPS):
            s = jnp.where(group == g, jnp.dot(band_ref[g], e, preferred_element_type=F32), s)
        p = s * inv_ref[rows, :] - e[H:H + CHUNK].astype(F32)
        y = jnp.dot(p.astype(BF16), w_ref[...], preferred_element_type=F32) * scale_ref[...]
        o_ref[rows, :] = (y * gate[rows]).astype(BF16)


def _layer_kernel(*refs, tm, first):
    n_stream = 2 if first else 14
    stream, rest = refs[:n_stream], refs[n_stream:]
    (g_ref, shift_ref, scl_ref, win_ref, sg_ref, sw_ref, sb_ref,
     band_ref, inv_ref, pw_ref, ps_ref) = rest[:11]
    outs = rest[11:]
    i = pl.program_id(1)
    n = pl.num_programs(1)
    H = POOL_HALO
    ext_ref, carry_ref = outs[-2:]
    slot_prev = (i + 1) % 2
    slot_this = i % 2

    @pl.when(i == 0)
    def _():
        carry_ref[slot_prev] = jnp.zeros((H, D_BRANCH), BF16)

    if first:
        x_ref, xh_ref = stream
        z_ref, ya_ref, yb_ref = outs[:3]
        h_ext = _modulate(jnp.concatenate([x_ref[...], xh_ref[...]], axis=0),
                          g_ref, shift_ref, scl_ref)
    else:
        ys, x_ref, ysh, xh_ref = stream[0:4], stream[4], stream[5:9], stream[9]
        cg_ref, cgh_ref, gprev_ref, wout_ref = stream[10:14]
        xo_ref, z_ref, ya_ref, yb_ref = outs[:4]
        ys_ext = [jnp.concatenate([m[...], mh[...]], axis=0) for m, mh in zip(ys, ysh)]
        ys_ext[2] = _gated(ys_ext[2], jnp.concatenate([cg_ref[...], cgh_ref[...]], axis=0))
        h_parts = []
        for r0 in range(0, tm, LAYER_ROWS):
            r1 = r0 + LAYER_ROWS
            if r1 == tm:
                r1 = tm + H
                x_in = jnp.concatenate([x_ref[r0:tm, :], xh_ref[...]], axis=0)
            else:
                x_in = x_ref[r0:r1, :]
            x = x_in + gprev_ref[...] * _mix_out([m[r0:r1] for m in ys_ext], wout_ref)
            xo_ref[r0:min(r1, tm), :] = x[0:min(r1, tm) - r0]
            h_parts.append(_modulate(x, g_ref, shift_ref, scl_ref))
        h_ext = jnp.concatenate(h_parts, axis=0)

    h = h_ext[0:tm]
    zl = jnp.dot(h_ext, win_ref[:, 0:Z_SLICES[0] * D_BRANCH], preferred_element_type=F32)

    def local(s):
        return zl[0:tm, s * D_BRANCH:(s + 1) * D_BRANCH]

    _sgu_body(local(B_U), local(B_V), local(B_GATE), sg_ref, sw_ref, sb_ref, yb_ref)

    a_ext = zl[:, A_IN * D_BRANCH:(A_IN + 1) * D_BRANCH].astype(BF16)
    a = a_ext[0:tm]
    a_next = a_ext[tm:tm + H]
    ext_ref[0:H, :] = carry_ref[slot_prev]
    ext_ref[H:H + tm, :] = a
    ext_ref[H + tm:2 * H + tm, :] = jnp.where(i < n - 1, a_next, jnp.zeros_like(a_next))
    carry_ref[slot_this] = a[tm - H:tm]
    _pool_body(ext_ref, local(A_GATE), band_ref, inv_ref, pw_ref, ps_ref, ya_ref)

    zg = jnp.dot(h, win_ref[:, Z_SLICES[0] * D_BRANCH:], preferred_element_type=F32)
    for k in range(len(Z_SLICES)):
        z_ref[k] = zg[:, k * D_BRANCH:(k + 1) * D_BRANCH].astype(BF16)


def _layer(ys, z_prev, x, gate_prev, w_out, norm_g, shift, scl, w_in, sgu_g, sgu_w, sgu_b, pool_w,
           pool_scale, tm=TM):
    B, L, _ = x.shape
    first = ys is None
    H = POOL_HALO
    nt = L // tm
    nh = L // H
    band, inv = _pool_tables(tm)

    def halo(width):
        return pl.BlockSpec((None, H, width),
                            lambda b, i: (b, jnp.minimum((i + 1) * (tm // H), nh - 1), 0))

    vec = pl.BlockSpec((None, 1, D_MODEL), lambda b, i: (b, 0, 0))
    if first:
        stream_specs = [_rows(tm, D_MODEL), halo(D_MODEL)]
        stream_args = [x, x]
    else:
        gate_halo = pl.BlockSpec(
            (None, None, H, D_BRANCH),
            lambda b, i: (ZC_GATE, b, jnp.minimum((i + 1) * (tm // H), nh - 1), 0))
        stream_specs = ([_rows(tm, D_BRANCH)] * 4 + [_rows(tm, D_MODEL)] + [halo(D_BRANCH)] * 4
                        + [halo(D_MODEL), _slab(ZC_GATE, tm), gate_halo, vec,
                           _const((D_MODEL, D_MODEL))])
        stream_args = [*ys, x, *ys, x, z_prev, z_prev, gate_prev, w_out]
    param_specs = [_const((1, D_MODEL)), vec, vec, _const((D_MODEL, D_IN)),
                   _const((1, D_BRANCH)), _const((N_GROUPS, CHUNK, CHUNK)), _const((CHUNK, D_BRANCH)),
                   _const(band.shape),
                   pl.BlockSpec((None, tm, D_BRANCH), lambda b, i: (_tile_variant(i, nt), 0, 0)),
                   _const((D_BRANCH, D_BRANCH)), _const((1, D_BRANCH))]
    param_args = [norm_g.reshape(1, D_MODEL), shift, scl, w_in,
                  sgu_g.reshape(1, D_BRANCH), sgu_w, sgu_b,
                  jnp.asarray(band).astype(BF16), jnp.asarray(inv), pool_w,
                  pool_scale.reshape(1, D_BRANCH)]
    nz = len(Z_SLICES)
    out_specs = [pl.BlockSpec((nz, None, tm, D_BRANCH), lambda b, i: (0, b, i, 0)),
                 _rows(tm, D_BRANCH), _rows(tm, D_BRANCH)]
    out_shape = [jax.ShapeDtypeStruct((nz, B, L, D_BRANCH), BF16),
                 jax.ShapeDtypeStruct((B, L, D_BRANCH), BF16),
                 jax.ShapeDtypeStruct((B, L, D_BRANCH), BF16)]
    if not first:
        out_specs = [_rows(tm, D_MODEL)] + out_specs
        out_shape = [jax.ShapeDtypeStruct((B, L, D_MODEL), F32)] + out_shape
    res = pl.pallas_call(
        functools.partial(_layer_kernel, tm=tm, first=first),
        grid=(B, nt),
        in_specs=stream_specs + param_specs,
        out_specs=out_specs,
        out_shape=out_shape,
        scratch_shapes=[pltpu.VMEM((tm + 2 * H, D_BRANCH), BF16),
                        pltpu.VMEM((2, H, D_BRANCH), BF16)],
        compiler_params=_params(2), name="layer_first" if first else "layer",
    )(*stream_args, *param_args)
    return (x, *res) if first else tuple(res)


def _dft_split(L):
    n2 = 128
    return L // n2, n2


def _dft_group(n1):
    return DFT_K // (2 * n1)


@functools.lru_cache(maxsize=None)
def _fnet_tables(L):
    n1, n2 = _dft_split(L)
    k1 = np.arange(n1, dtype=np.int64)
    n = np.arange(n2, dtype=np.int64)[:, None, None] + n2 * np.arange(n1, dtype=np.int64)[None, None, :]
    ang = 2.0 * np.pi * ((k1[None, :, None] * n) % L).astype(np.float64) / L
    gc, gs = np.cos(ang), np.sin(ang)
    gg = np.concatenate([np.concatenate([gc, -gs], axis=2),
                         np.concatenate([-gs, -gc], axis=2)], axis=1)
    jb = _dft_group(n1)
    grouped = np.zeros((n2 // jb, jb, 2 * n1, jb, 2 * n1))
    for j in range(jb):
        grouped[:, j, :, j, :] = gg[j::jb]
    gg = grouped.reshape(n2 // jb, DFT_K, DFT_K)
    kk = np.arange(n2, dtype=np.int64)
    ang2 = 2.0 * np.pi * ((kk[:, None] * kk[None, :]) % n2).astype(np.float64) / n2
    cs2 = np.concatenate([np.cos(ang2), np.sin(ang2)], axis=1)
    return gg.astype(np.float32), cs2.astype(np.float32)


@functools.lru_cache(maxsize=None)
def _channel_tables():
    c = np.arange(D_GROUP, dtype=np.int64)
    ang = 2.0 * np.pi * ((c[:, None] * c[None, :]) % D_GROUP).astype(np.float64) / D_GROUP
    eye = np.eye(N_GROUPS)
    return (np.kron(eye, np.cos(ang)).astype(np.float32),
            np.kron(eye, np.sin(ang)).astype(np.float32))


def _block_diag(w):
    eye = jnp.eye(N_GROUPS, dtype=w.dtype)
    out = jnp.einsum('...gcd,gh->...gchd', w, eye)
    return out.reshape(*w.shape[:-3], D_BRANCH, D_BRANCH)


def _fold_kernel(bdc_ref, bds_ref, w_ref, mc_ref, ms_ref):
    w = w_ref[...]
    hi = lax.Precision.HIGHEST
    mc_ref[...] = jnp.dot(bdc_ref[...], w, precision=hi, preferred_element_type=F32).astype(BF16)
    ms_ref[...] = jnp.dot(bds_ref[...], w, precision=hi, preferred_element_type=F32).astype(BF16)


def _fnet_fold(fnet_w_bd):
    bdc, bds = _channel_tables()
    full = pl.BlockSpec((D_BRANCH, D_BRANCH), lambda l: (0, 0))
    per = pl.BlockSpec((None, D_BRANCH, D_BRANCH), lambda l: (l, 0, 0))
    shp = jax.ShapeDtypeStruct((DEPTH, D_BRANCH, D_BRANCH), BF16)
    return pl.pallas_call(
        _fold_kernel, grid=(DEPTH,), in_specs=[full, full, per], out_specs=[per, per],
        out_shape=[shp, shp], compiler_params=_params(1), name="fnet_fold",
    )(jnp.asarray(bdc), jnp.asarray(bds), fnet_w_bd)


def _fnet1_kernel(x_ref, gg_ref, mc_ref, ms_ref, a_ref, *, n1, t):
    jb = _dft_group(n1)
    xt = jnp.swapaxes(x_ref[...].astype(F32), 0, 1)
    x2 = xt.reshape(t * n1, D_BRANCH).astype(BF16)
    u = jnp.dot(x2, mc_ref[...], preferred_element_type=F32)
    v = jnp.dot(x2, ms_ref[...], preferred_element_type=F32)
    groups = []
    for g in range(t // jb):
        parts = []
        for j in range(g * jb, (g + 1) * jb):
            parts += [u[j * n1:(j + 1) * n1], v[j * n1:(j + 1) * n1]]
        uv = jnp.concatenate(parts, axis=0).astype(BF16)
        a = jnp.dot(gg_ref[g], uv, preferred_element_type=F32)
        groups.append(a.reshape(jb, 2 * n1, D_BRANCH))
    a_ref[...] = jnp.swapaxes(jnp.concatenate(groups, axis=0), 0, 1).astype(BF16)


def _fnet2_kernel(ar_ref, ai_ref, cs_ref, o_ref, *, norm):
    ys = []
    for j in range(ar_ref.shape[0]):
        a = jnp.concatenate([ar_ref[j], ai_ref[j]], axis=0)
        ys.append(jnp.dot(cs_ref[...], a, preferred_element_type=F32))
    y = jnp.swapaxes(jnp.stack(ys, axis=0), 0, 1)
    o_ref[...] = (y * norm).astype(BF16)


def _fnet(z, mc, ms):
    nz, B, L, _ = z.shape
    n1, n2 = _dft_split(L)
    T = DFT_T
    assert L == n1 * n2 and n1 % T == 0 and DFT_K % (2 * n1) == 0, L
    gg, cs2 = _fnet_tables(L)
    gg, cs2 = jnp.asarray(gg).astype(BF16), jnp.asarray(cs2).astype(BF16)
    full = pl.BlockSpec((D_BRANCH, D_BRANCH), lambda b, i: (0, 0))
    t1 = min(n2, max(T, DFT_TOKENS // n1))
    a = pl.pallas_call(
        functools.partial(_fnet1_kernel, n1=n1, t=t1),
        grid=(B, n2 // t1),
        in_specs=[pl.BlockSpec((None, None, n1, t1, D_BRANCH), lambda b, i: (ZC_IN, b, 0, i, 0)),
                  pl.BlockSpec((t1 // _dft_group(n1), DFT_K, DFT_K), lambda b, i: (i, 0, 0)),
                  full, full],
        out_specs=pl.BlockSpec((None, 2 * n1, t1, D_BRANCH), lambda b, i: (b, 0, i, 0)),
        out_shape=jax.ShapeDtypeStruct((B, 2 * n1, n2, D_BRANCH), BF16),
        compiler_params=_params(2), name="fnet_stage1",
    )(z.reshape(nz, B, n1, n2, D_BRANCH), gg, mc, ms)

    t2 = min(n1, DFT_TOKENS * 2 // n2)
    nk = n1 // t2
    y = pl.pallas_call(
        functools.partial(_fnet2_kernel, norm=float(1.0 / np.sqrt(L * D_GROUP))),
        grid=(B, nk),
        in_specs=[pl.BlockSpec((None, t2, n2, D_BRANCH), lambda b, i: (b, i, 0, 0)),
                  pl.BlockSpec((None, t2, n2, D_BRANCH), lambda b, i: (b, nk + i, 0, 0)),
                  pl.BlockSpec((n2, 2 * n2), lambda b, i: (0, 0))],
        out_specs=pl.BlockSpec((None, n2, t2, D_BRANCH), lambda b, i: (b, 0, i, 0)),
        out_shape=jax.ShapeDtypeStruct((B, n2, n1, D_BRANCH), BF16),
        compiler_params=_params(2), name="fnet_stage2",
    )(a, a, cs2)
    return y.reshape(B, L, D_BRANCH)


def _na_key_col0(cb):
    return int(np.clip(cb * NA_CQ - NA_KW // 2, 0, GRID_W - NA_CK))


@functools.lru_cache(maxsize=None)
def _na_tables():
    i = np.arange(NA_TR)[:, None]
    j = np.arange(NA_KROWS)[None, :]
    row_sel = np.zeros((3, NA_TR, NA_KROWS, 2 * NA_KH - 1), np.float32)
    row_ok = np.zeros((3, NA_TR, NA_KROWS), bool)
    for var, (base, rel) in enumerate(((0, np.zeros(NA_TR, int)), (-NA_TR, np.arange(NA_TR)),
                                       (-NA_KH, np.full(NA_TR, NA_KROWS - NA_KH)))):
        ok = (j >= rel[:, None]) & (j < rel[:, None] + NA_KH)
        dr = base + j - i + NA_KH - 1
        for ii, jj in zip(*np.nonzero(ok)):
            row_sel[var, ii, jj, dr[ii, jj]] = 1.0
        row_ok[var] = ok
    col_sel = np.zeros((NA_NCB, NA_CQ, NA_CK, 2 * NA_KW - 1), np.float32)
    col_ok = np.zeros((NA_NCB, NA_CQ, NA_CK), bool)
    for cb in range(NA_NCB):
        for cq in range(NA_CQ):
            c = cb * NA_CQ + cq
            cst = int(np.clip(c - NA_KW // 2, 0, GRID_W - NA_KW))
            for kl in range(NA_CK):
                kc = _na_key_col0(cb) + kl
                if cst <= kc < cst + NA_KW:
                    col_sel[cb, cq, kl, kc - c + NA_KW - 1] = 1.0
                    col_ok[cb, cq, kl] = True
    ok = row_ok[:, None, :, None, :, None] & col_ok[None, :, None, :, None, :]
    mask = np.where(ok, 0.0, NEG_BIG).astype(np.float32).reshape(3, NA_NCB, 1, NA_BQ, NA_BK)
    nb = 2 * NA_KW - 1
    col_sel_g = np.zeros((NA_JG, nb, NA_NCB, NA_CQ, NA_JG, NA_CK), np.float32)
    for j in range(NA_JG):
        col_sel_g[j, :, :, :, j, :] = np.transpose(col_sel, (3, 0, 1, 2))
    return row_sel, col_sel_g.reshape(NA_JG * nb, NA_NCB * NA_CQ * NA_JG * NA_CK), mask


def _na_bias(na_rpb):
    row_sel, col_sel_g, mask = _na_tables()
    hi = lax.Precision.HIGHEST
    ng = NA_KROWS // NA_JG
    b = jnp.einsum('vija,lhab->lvhijb', row_sel, na_rpb, precision=hi)
    b = b.reshape(DEPTH * 3 * N_GROUPS * NA_TR, ng, col_sel_g.shape[0])
    b = jnp.einsum('rgx,xn->rgn', b, col_sel_g, precision=hi)
    b = b.reshape(DEPTH, 3, N_GROUPS, NA_TR, ng, NA_NCB, NA_CQ, NA_JG * NA_CK)
    b = jnp.transpose(b, (0, 1, 5, 2, 3, 6, 4, 7))
    b = b.reshape(DEPTH, 3, NA_NCB, N_GROUPS, NA_BQ, NA_BK) + mask
    return b.reshape(DEPTH, 3, NA_NCB, N_GROUPS * NA_BQ, NA_BK)


def _na_kernel(*refs):
    q_ref, gate_ref = refs[:2]
    o_ref, acc_ref = refs[-2:]
    for t in range(NA_SUB):
        tile = refs[2 + 7 * t:9 + 7 * t]
        _na_tile(q_ref, t * NA_TQ, tile[0:3], tile[3:6], tile[6], acc_ref)
    o_ref[...] = (acc_ref[...] * _silu(gate_ref[...].astype(F32))).astype(BF16)


def _na_tile(q_ref, row0, k_refs, v_refs, bias_ref, acc_ref):
    head = lax.broadcasted_iota(jnp.int32, (NA_BQ, D_BRANCH), 1) // D_GROUP
    k = jnp.concatenate([r[...] for r in k_refs], axis=0).astype(F32)
    v = jnp.concatenate([r[...] for r in v_refs], axis=0).astype(F32)
    for cb in range(NA_NCB):
        c0 = row0 + cb * NA_CQ
        kc0 = _na_key_col0(cb)
        q = jnp.concatenate([q_ref[i * GRID_W + c0:i * GRID_W + c0 + NA_CQ, :]
                             for i in range(NA_TR)], axis=0)
        q = q * jnp.asarray(D_GROUP ** -0.5, BF16)
        zero = jnp.zeros_like(q)
        qs = jnp.concatenate([jnp.where(head == h, q, zero) for h in range(N_GROUPS)], axis=0)
        kb = jnp.concatenate([k[j * GRID_W + kc0:j * GRID_W + kc0 + NA_CK]
                              for j in range(NA_KROWS)], axis=0).astype(BF16)
        vb = jnp.concatenate([v[j * GRID_W + kc0:j * GRID_W + kc0 + NA_CK]
                              for j in range(NA_KROWS)], axis=0).astype(BF16)
        s = lax.dot_general(qs, kb, (((1,), (1,)), ((), ())), preferred_element_type=F32)
        s = s + bias_ref[cb]
        m = jnp.max(s, axis=-1, keepdims=True)
        p = jnp.exp(s - m)
        denom = jnp.sum(p, axis=-1, keepdims=True)
        o = jnp.dot(p.astype(BF16), vb, preferred_element_type=F32) / denom
        out = o[0:NA_BQ]
        for h in range(1, N_GROUPS):
            out = jnp.where(head == h, o[h * NA_BQ:(h + 1) * NA_BQ], out)
        for i in range(NA_TR):
            acc_ref[i * GRID_W + c0:i * GRID_W + c0 + NA_CQ, :] = out[i * NA_CQ:(i + 1) * NA_CQ]


def _na(z, bias):
    _, B, L, _ = z.shape
    nt = L // NA_TQ
    nkb = NA_KROWS // NA_TR
    assert nt % NA_SUB == 0 and nt >= nkb, L

    def kv(s, t, d):
        return pl.BlockSpec(
            (None, None, NA_TQ, D_BRANCH),
            lambda b, i: (s, b, jnp.clip(i * NA_SUB + t - 1, 0, nt - nkb) + d, 0))

    tile_specs, tile_args = [], []
    for t in range(NA_SUB):
        tile_specs += [kv(ZD_K, t, d) for d in range(nkb)] + [kv(ZD_V, t, d) for d in range(nkb)]
        tile_specs.append(pl.BlockSpec(
            (None, NA_NCB, N_GROUPS * NA_BQ, NA_BK),
            lambda b, i, t=t: (_tile_variant(i * NA_SUB + t, nt), 0, 0, 0)))
        tile_args += [z] * (2 * nkb) + [bias]
    return pl.pallas_call(
        _na_kernel,
        grid=(B, nt // NA_SUB),
        in_specs=[_slab(ZD_Q, NA_SUB * NA_TQ), _slab(ZD_GATE, NA_SUB * NA_TQ)] + tile_specs,
        out_specs=_rows(NA_SUB * NA_TQ, D_BRANCH),
        out_shape=jax.ShapeDtypeStruct((B, L, D_BRANCH), BF16),
        scratch_shapes=[pltpu.VMEM((NA_SUB * NA_TQ, D_BRANCH), F32)],
        compiler_params=_params(2), name="na",
    )(z, z, *tile_args)


def _final_kernel(ya_ref, yb_ref, yc_ref, yd_ref, cg_ref, x_ref, gate_ref, w_ref, fg_ref, o_ref):
    y = _mix_out([ya_ref[...], yb_ref[...], _gated(yc_ref[...], cg_ref[...]), yd_ref[...]], w_ref)
    o_ref[...] = _rms(x_ref[...] + gate_ref[...] * y) * fg_ref[...]


def _final(ys, z_prev, x, gate, w_bf, final_g, tm=TM):
    B, L, _ = x.shape
    return pl.pallas_call(
        _final_kernel,
        grid=(B, L // tm),
        in_specs=[_rows(tm, D_BRANCH)] * 4 + [
            _slab(ZC_GATE, tm),
            _rows(tm, D_MODEL), pl.BlockSpec((None, 1, D_MODEL), lambda b, i: (b, 0, 0)),
            _const((D_MODEL, D_MODEL)), _const((1, D_MODEL))],
        out_specs=_rows(tm, D_MODEL),
        out_shape=jax.ShapeDtypeStruct((B, L, D_MODEL), F32),
        compiler_params=_params(2), name="final",
    )(*ys, z_prev, x, gate, w_bf, final_g.reshape(1, D_MODEL))


def _trunk(x, mod, prep, norm_g, pool_scale, sgu_norm_g, final_norm_g):
    B = x.shape[0]
    ys, z, gate = None, None, None
    for l in range(DEPTH):
        shift, scl, gate_l = (mod[l, :, n * D_MODEL:(n + 1) * D_MODEL].reshape(B, 1, D_MODEL)
                              for n in range(3))
        x, z, ya, yb = _layer(ys, z, x, gate, prep["w_out"][l - 1], norm_g[l], shift, scl,
                              prep["w_in"][l], sgu_norm_g[l], prep["sgu_w"][l], prep["sgu_b"][l],
                              prep["pool_w"][l], pool_scale[l])
        yc = _fnet(z, prep["fnet_mc"][l], prep["fnet_ms"][l])
        yd = _na(z, prep["na_bias"][l])
        ys, gate = (ya, yb, yc, yd), gate_l
    return _final(ys, z, x, gate, prep["w_out"][DEPTH - 1], final_norm_g)


def kernel(x_prompt, x_sample, c_prompt, c_sample, norm_g, w_ada, b_ada, w_in, w_out, pool_w, pool_scale, sgu_norm_g, sgu_w, sgu_b, fnet_w, na_rpb, final_norm_g):
    nb_p, nb_s = c_prompt.shape[0], c_sample.shape[0]
    c_all = jnp.concatenate(
        [c_prompt, c_sample, jnp.zeros((C_PAD - nb_p - nb_s, D_MODEL), F32)], axis=0)
    mod = _adaln(c_all, w_ada, b_ada)

    mc, ms = _fnet_fold(_block_diag(fnet_w))
    prep = {
        "w_in": w_in.astype(BF16),
        "w_out": w_out.astype(BF16),
        "pool_w": _block_diag(pool_w).astype(BF16),
        "sgu_w": sgu_w.astype(BF16),
        "sgu_b": jnp.repeat(jnp.swapaxes(sgu_b, 1, 2), D_GROUP, axis=2),
        "fnet_mc": mc,
        "fnet_ms": ms,
        "na_bias": _na_bias(na_rpb),
    }
    y_prompt = _trunk(x_prompt, mod[:, :nb_p], prep, norm_g, pool_scale, sgu_norm_g, final_norm_g)
    y_sample = _trunk(x_sample, mod[:, nb_p:nb_p + nb_s], prep, norm_g, pool_scale, sgu_norm_g,
                      final_norm_g)
    return (y_prompt, y_sample)
```

```python
import functools

import numpy as np
import jax
import jax.numpy as jnp
from jax import lax
from jax.experimental import pallas as pl
from jax.experimental.pallas import tpu as pltpu

F32 = jnp.float32
BF16 = jnp.bfloat16
BF16_ROWS = 16

D_MODEL = 1024
DEPTH = 4
D_BRANCH = 256
N_GROUPS = 4
D_GROUP = 64
POOL_WINDOWS = (2, 4, 8, 16)
POOL_HALO = BF16_ROWS
CHUNK = 128
GRID_W = 64
NA_KH = 8
NA_KW = 16
N_IN_SLICES = 11
D_IN = N_IN_SLICES * D_BRANCH
RMS_EPS = 1e-6
LN_EPS = 1e-5
NEG_BIG = -1e30

A_IN, A_GATE, B_U, B_V, B_GATE, C_IN, C_GATE, D_Q, D_K, D_V, D_GATE = range(N_IN_SLICES)
Z_SLICES = (C_IN, C_GATE, D_Q, D_K, D_V, D_GATE)
ZC_IN, ZC_GATE, ZD_Q, ZD_K, ZD_V, ZD_GATE = range(len(Z_SLICES))

NA_TR = 4
NA_KROWS = NA_TR + NA_KH
NA_TQ = NA_TR * GRID_W
NA_SUB = 4
NA_CQ = 16
NA_NCB = GRID_W // NA_CQ
NA_CK = 2 * NA_KW
NA_BQ = NA_TR * NA_CQ
NA_BK = NA_KROWS * NA_CK
NA_JG = 128 // NA_CK
C_PAD = 16
DFT_T = BF16_ROWS
DFT_K = 256
DFT_TOKENS = 2048
TM = 1024
LAYER_ROWS = 256

VMEM_LIMIT = 56 * 1024 * 1024


def _params(n_axes, vmem=VMEM_LIMIT):
    return pltpu.CompilerParams(dimension_semantics=("arbitrary",) * n_axes,
                                vmem_limit_bytes=vmem)


def _silu(x):
    return x / (1.0 + jnp.exp(-x))


def _gelu(x):
    return x * (0.5 * (1.0 + jnp.tanh(np.sqrt(2.0 / np.pi) * (x + 0.044715 * (x * x * x)))))


def _rms(x):
    return x * lax.rsqrt(jnp.mean(x * x, axis=-1, keepdims=True) + RMS_EPS)


def _tile_variant(i, n):
    return jnp.where(i == 0, 0, jnp.where(i == n - 1, 2, 1))


def _const(shape):
    zeros = (0,) * len(shape)
    return pl.BlockSpec(shape, lambda b, i: zeros, pipeline_mode=pl.Buffered(1))


def _rows(t, width):
    return pl.BlockSpec((None, t, width), lambda b, i: (b, i, 0))


def _slab(s, t):
    return pl.BlockSpec((None, None, t, D_BRANCH), lambda b, i: (s, b, i, 0))


def _adaln_kernel(c_ref, w_ref, b_ref, o_ref):
    s = _silu(c_ref[...])
    o_ref[...] = jnp.dot(s, w_ref[...], precision=lax.Precision.HIGHEST,
                         preferred_element_type=F32) + b_ref[...]


def _adaln(c_all, w_ada, b_ada):
    return pl.pallas_call(
        _adaln_kernel,
        grid=(DEPTH, 3),
        in_specs=[pl.BlockSpec((C_PAD, D_MODEL), lambda l, n: (0, 0)),
                  pl.BlockSpec((None, D_MODEL, D_MODEL), lambda l, n: (l, 0, n)),
                  pl.BlockSpec((None, 1, D_MODEL), lambda l, n: (l, 0, n))],
        out_specs=pl.BlockSpec((None, C_PAD, D_MODEL), lambda l, n: (l, 0, n)),
        out_shape=jax.ShapeDtypeStruct((DEPTH, C_PAD, 3 * D_MODEL), F32),
        compiler_params=_params(2), name="adaln",
    )(c_all, w_ada, b_ada.reshape(DEPTH, 1, 3 * D_MODEL))


@functools.lru_cache(maxsize=None)
def _pool_tables(tp):
    r = np.arange(CHUNK)[:, None]
    c = np.arange(CHUNK + 2 * POOL_HALO)[None, :]
    band = np.stack([(c >= r + POOL_HALO - w // 2) & (c < r + POOL_HALO + w // 2)
                     for w in POOL_WINDOWS]).astype(np.float32)
    t = np.arange(tp)[:, None]
    w = np.repeat(np.array(POOL_WINDOWS), D_GROUP)[None, :]
    head = np.minimum(t + w // 2, tp + w) - np.maximum(t - w // 2, 0)
    tail = np.minimum(t + w // 2, tp) - np.maximum(t - w // 2, -w)
    inv = np.stack([1.0 / head, 1.0 / np.broadcast_to(w, head.shape), 1.0 / tail])
    return band, inv.astype(np.float32)


def _gated(y, gate):
    return (y.astype(F32) * _silu(gate.astype(F32))).astype(BF16)


def _mix_out(ys, w_ref):
    y = jnp.dot(ys[0], w_ref[0:D_BRANCH, :], preferred_element_type=F32)
    for n in range(1, len(ys)):
        y = y + jnp.dot(ys[n], w_ref[n * D_BRANCH:(n + 1) * D_BRANCH, :],
                        preferred_element_type=F32)
    return y


def _modulate(x, g_ref, shift_ref, scl_ref):
    return ((_rms(x) * g_ref[...]) * (1.0 + scl_ref[...]) + shift_ref[...]).astype(BF16)


def _project(h, w_ref, s):
    return jnp.dot(h, w_ref[:, s * D_BRANCH:(s + 1) * D_BRANCH], preferred_element_type=F32)


def _sgu_body(u, v, gate, g_ref, w_ref, b_ref, o_ref):
    t = u.shape[0]
    u = _gelu(u)
    v = _gelu(v)
    mu = jnp.mean(v, axis=-1, keepdims=True)
    d = v - mu
    var = jnp.mean(d * d, axis=-1, keepdims=True)
    vn = (d * lax.rsqrt(var + LN_EPS) * g_ref[...]).astype(BF16)
    gate = _silu(gate)
    head = lax.broadcasted_iota(jnp.int32, (CHUNK, D_BRANCH), 1) // D_GROUP
    for n in range(t // CHUNK):
        rows = slice(n * CHUNK, (n + 1) * CHUNK)
        vc = vn[rows]
        s = jnp.dot(w_ref[0], vc, preferred_element_type=F32)
        for h in range(1, N_GROUPS):
            s = jnp.where(head == h, jnp.dot(w_ref[h], vc, preferred_element_type=F32), s)
        o_ref[rows, :] = ((u[rows] * (s + b_ref[...])) * gate[rows]).astype(BF16)


def _pool_body(ext_ref, gate, band_ref, inv_ref, w_ref, scale_ref, o_ref):
    t = gate.shape[0]
    H = POOL_HALO
    gate = _silu(gate)
    group = lax.broadcasted_iota(jnp.int32, (CHUNK, D_BRANCH), 1) // D_GROUP
    for m in range(t // CHUNK):
        rows = slice(m * CHUNK, (m + 1) * CHUNK)
        e = ext_ref[m * CHUNK:(m + 1) * CHUNK + 2 * H, :]
        s = jnp.dot(band_ref[0], e, preferred_element_type=F32)
        for g in range(1, N_GROUPS):
            s = jnp.where(group == g, jnp.dot(band_ref[g], e, preferred_element_type=F32), s)
        p = s * inv_ref[rows, :] - e[H:H + CHUNK].astype(F32)
        y = jnp.dot(p.astype(BF16), w_ref[...], preferred_element_type=F32) * scale_ref[...]
        o_ref[rows, :] = (y * gate[rows]).astype(BF16)


def _layer_kernel(*refs, tm, first):
    n_stream = 2 if first else 14
    stream, rest = refs[:n_stream], refs[n_stream:]
    (g_ref, shift_ref, scl_ref, win_ref, sg_ref, sw_ref, sb_ref,
     band_ref, inv_ref, pw_ref, ps_ref) = rest[:11]
    outs = rest[11:]
    i = pl.program_id(1)
    n = pl.num_programs(1)
    H = POOL_HALO
    ext_ref, carry_ref = outs[-2:]
    slot_prev = (i + 1) % 2
    slot_this = i % 2

    @pl.when(i == 0)
    def _():
        carry_ref[slot_prev] = jnp.zeros((H, D_BRANCH), BF16)

    if first:
        x_ref, xh_ref = stream
        z_ref, ya_ref, yb_ref = outs[:3]
        h_ext = _modulate(jnp.concatenate([x_ref[...], xh_ref[...]], axis=0),
                          g_ref, shift_ref, scl_ref)
    else:
        ys, x_ref, ysh, xh_ref = stream[0:4], stream[4], stream[5:9], stream[9]
        cg_ref, cgh_ref, gprev_ref, wout_ref = stream[10:14]
        xo_ref, z_ref, ya_ref, yb_ref = outs[:4]
        ys_ext = [jnp.concatenate([m[...], mh[...]], axis=0) for m, mh in zip(ys, ysh)]
        ys_ext[2] = _gated(ys_ext[2], jnp.concatenate([cg_ref[...], cgh_ref[...]], axis=0))
        h_parts = []
        for r0 in range(0, tm, LAYER_ROWS):
            r1 = r0 + LAYER_ROWS
            if r1 == tm:
                r1 = tm + H
                x_in = jnp.concatenate([x_ref[r0:tm, :], xh_ref[...]], axis=0)
            else:
                x_in = x_ref[r0:r1, :]
            x = x_in + gprev_ref[...] * _mix_out([m[r0:r1] for m in ys_ext], wout_ref)
            xo_ref[r0:min(r1, tm), :] = x[0:min(r1, tm) - r0]
            h_parts.append(_modulate(x, g_ref, shift_ref, scl_ref))
        h_ext = jnp.concatenate(h_parts, axis=0)

    h = h_ext[0:tm]
    zl = jnp.dot(h_ext, win_ref[:, 0:Z_SLICES[0] * D_BRANCH], preferred_element_type=F32)

    def local(s):
        return zl[0:tm, s * D_BRANCH:(s + 1) * D_BRANCH]

    _sgu_body(local(B_U), local(B_V), local(B_GATE), sg_ref, sw_ref, sb_ref, yb_ref)

    a_ext = zl[:, A_IN * D_BRANCH:(A_IN + 1) * D_BRANCH].astype(BF16)
    a = a_ext[0:tm]
    a_next = a_ext[tm:tm + H]
    ext_ref[0:H, :] = carry_ref[slot_prev]
    ext_ref[H:H + tm, :] = a
    ext_ref[H + tm:2 * H + tm, :] = jnp.where(i < n - 1, a_next, jnp.zeros_like(a_next))
    carry_ref[slot_this] = a[tm - H:tm]
    _pool_body(ext_ref, local(A_GATE), band_ref, inv_ref, pw_ref, ps_ref, ya_ref)

    zg = jnp.dot(h, win_ref[:, Z_SLICES[0] * D_BRANCH:], preferred_element_type=F32)
    for k in range(len(Z_SLICES)):
        z_ref[k] = zg[:, k * D_BRANCH:(k + 1) * D_BRANCH].astype(BF16)


def _layer(ys, z_prev, x, gate_prev, w_out, norm_g, shift, scl, w_in, sgu_g, sgu_w, sgu_b, pool_w,
           pool_scale, tm=TM):
    B, L, _ = x.shape
    first = ys is None
    H = POOL_HALO
    nt = L // tm
    nh = L // H
    band, inv = _pool_tables(tm)

    def halo(width):
        return pl.BlockSpec((None, H, width),
                            lambda b, i: (b, jnp.minimum((i + 1) * (tm // H), nh - 1), 0))

    vec = pl.BlockSpec((None, 1, D_MODEL), lambda b, i: (b, 0, 0))
    if first:
        stream_specs = [_rows(tm, D_MODEL), halo(D_MODEL)]
        stream_args = [x, x]
    else:
        gate_halo = pl.BlockSpec(
            (None, None, H, D_BRANCH),
            lambda b, i: (ZC_GATE, b, jnp.minimum((i + 1) * (tm // H), nh - 1), 0))
        stream_specs = ([_rows(tm, D_BRANCH)] * 4 + [_rows(tm, D_MODEL)] + [halo(D_BRANCH)] * 4
                        + [halo(D_MODEL), _slab(ZC_GATE, tm), gate_halo, vec,
                           _const((D_MODEL, D_MODEL))])
        stream_args = [*ys, x, *ys, x, z_prev, z_prev, gate_prev, w_out]
    param_specs = [_const((1, D_MODEL)), vec, vec, _const((D_MODEL, D_IN)),
                   _const((1, D_BRANCH)), _const((N_GROUPS, CHUNK, CHUNK)), _const((CHUNK, D_BRANCH)),
                   _const(band.shape),
                   pl.BlockSpec((None, tm, D_BRANCH), lambda b, i: (_tile_variant(i, nt), 0, 0)),
                   _const((D_BRANCH, D_BRANCH)), _const((1, D_BRANCH))]
    param_args = [norm_g.reshape(1, D_MODEL), shift, scl, w_in,
                  sgu_g.reshape(1, D_BRANCH), sgu_w, sgu_b,
                  jnp.asarray(band).astype(BF16), jnp.asarray(inv), pool_w,
                  pool_scale.reshape(1, D_BRANCH)]
    nz = len(Z_SLICES)
    out_specs = [pl.BlockSpec((nz, None, tm, D_BRANCH), lambda b, i: (0, b, i, 0)),
                 _rows(tm, D_BRANCH), _rows(tm, D_BRANCH)]
    out_shape = [jax.ShapeDtypeStruct((nz, B, L, D_BRANCH), BF16),
                 jax.ShapeDtypeStruct((B, L, D_BRANCH), BF16),
                 jax.ShapeDtypeStruct((B, L, D_BRANCH), BF16)]
    if not first:
        out_specs = [_rows(tm, D_MODEL)] + out_specs
        out_shape = [jax.ShapeDtypeStruct((B, L, D_MODEL), F32)] + out_shape
    res = pl.pallas_call(
        functools.partial(_layer_kernel, tm=tm, first=first),
        grid=(B, nt),
        in_specs=stream_specs + param_specs,
        out_specs=out_specs,
        out_shape=out_shape,
        scratch_shapes=[pltpu.VMEM((tm + 2 * H, D_BRANCH), BF16),
                        pltpu.VMEM((2, H, D_BRANCH), BF16)],
        compiler_params=_params(2), name="layer_first" if first else "layer",
    )(*stream_args, *param_args)
    return (x, *res) if first else tuple(res)


def _dft_split(L):
    n2 = 128
    return L // n2, n2


def _dft_group(n1):
    return DFT_K // (2 * n1)


@functools.lru_cache(maxsize=None)
def _fnet_tables(L):
    n1, n2 = _dft_split(L)
    k1 = np.arange(n1, dtype=np.int64)
    n = np.arange(n2, dtype=np.int64)[:, None, None] + n2 * np.arange(n1, dtype=np.int64)[None, None, :]
    ang = 2.0 * np.pi * ((k1[None, :, None] * n) % L).astype(np.float64) / L
    gc, gs = np.cos(ang), np.sin(ang)
    gg = np.concatenate([np.concatenate([gc, -gs], axis=2),
                         np.concatenate([-gs, -gc], axis=2)], axis=1)
    jb = _dft_group(n1)
    grouped = np.zeros((n2 // jb, jb, 2 * n1, jb, 2 * n1))
    for j in range(jb):
        grouped[:, j, :, j, :] = gg[j::jb]
    gg = grouped.reshape(n2 // jb, DFT_K, DFT_K)
    kk = np.arange(n2, dtype=np.int64)
    ang2 = 2.0 * np.pi * ((kk[:, None] * kk[None, :]) % n2).astype(np.float64) / n2
    cs2 = np.concatenate([np.cos(ang2), np.sin(ang2)], axis=1)
    return gg.astype(np.float32), cs2.astype(np.float32)


@functools.lru_cache(maxsize=None)
def _channel_tables():
    c = np.arange(D_GROUP, dtype=np.int64)
    ang = 2.0 * np.pi * ((c[:, None] * c[None, :]) % D_GROUP).astype(np.float64) / D_GROUP
    eye = np.eye(N_GROUPS)
    return (np.kron(eye, np.cos(ang)).astype(np.float32),
            np.kron(eye, np.sin(ang)).astype(np.float32))


def _block_diag(w):
    eye = jnp.eye(N_GROUPS, dtype=w.dtype)
    out = jnp.einsum('...gcd,gh->...gchd', w, eye)
    return out.reshape(*w.shape[:-3], D_BRANCH, D_BRANCH)


def _fold_kernel(bdc_ref, bds_ref, w_ref, mc_ref, ms_ref):
    w = w_ref[...]
    hi = lax.Precision.HIGHEST
    mc_ref[...] = jnp.dot(bdc_ref[...], w, precision=hi, preferred_element_type=F32).astype(BF16)
    ms_ref[...] = jnp.dot(bds_ref[...], w, precision=hi, preferred_element_type=F32).astype(BF16)


def _fnet_fold(fnet_w_bd):
    bdc, bds = _channel_tables()
    full = pl.BlockSpec((D_BRANCH, D_BRANCH), lambda l: (0, 0))
    per = pl.BlockSpec((None, D_BRANCH, D_BRANCH), lambda l: (l, 0, 0))
    shp = jax.ShapeDtypeStruct((DEPTH, D_BRANCH, D_BRANCH), BF16)
    return pl.pallas_call(
        _fold_kernel, grid=(DEPTH,), in_specs=[full, full, per], out_specs=[per, per],
        out_shape=[shp, shp], compiler_params=_params(1), name="fnet_fold",
    )(jnp.asarray(bdc), jnp.asarray(bds), fnet_w_bd)


def _fnet1_kernel(x_ref, gg_ref, mc_ref, ms_ref, a_ref, *, n1, t):
    jb = _dft_group(n1)
    xt = jnp.swapaxes(x_ref[...].astype(F32), 0, 1)
    x2 = xt.reshape(t * n1, D_BRANCH).astype(BF16)
    u = jnp.dot(x2, mc_ref[...], preferred_element_type=F32)
    v = jnp.dot(x2, ms_ref[...], preferred_element_type=F32)
    groups = []
    for g in range(t // jb):
        parts = []
        for j in range(g * jb, (g + 1) * jb):
            parts += [u[j * n1:(j + 1) * n1], v[j * n1:(j + 1) * n1]]
        uv = jnp.concatenate(parts, axis=0).astype(BF16)
        a = jnp.dot(gg_ref[g], uv, preferred_element_type=F32)
        groups.append(a.reshape(jb, 2 * n1, D_BRANCH))
    a_ref[...] = jnp.swapaxes(jnp.concatenate(groups, axis=0), 0, 1).astype(BF16)


def _fnet2_kernel(ar_ref, ai_ref, cs_ref, o_ref, *, norm):
    ys = []
    for j in range(ar_ref.shape[0]):
        a = jnp.concatenate([ar_ref[j], ai_ref[j]], axis=0)
        ys.append(jnp.dot(cs_ref[...], a, preferred_element_type=F32))
    y = jnp.swapaxes(jnp.stack(ys, axis=0), 0, 1)
    o_ref[...] = (y * norm).astype(BF16)


def _fnet(z, mc, ms):
    nz, B, L, _ = z.shape
    n1, n2 = _dft_split(L)
    T = DFT_T
    assert L == n1 * n2 and n1 % T == 0 and DFT_K % (2 * n1) == 0, L
    gg, cs2 = _fnet_tables(L)
    gg, cs2 = jnp.asarray(gg).astype(BF16), jnp.asarray(cs2).astype(BF16)
    full = pl.BlockSpec((D_BRANCH, D_BRANCH), lambda b, i: (0, 0))
    t1 = min(n2, max(T, DFT_TOKENS // n1))
    a = pl.pallas_call(
        functools.partial(_fnet1_kernel, n1=n1, t=t1),
        grid=(B, n2 // t1),
        in_specs=[pl.BlockSpec((None, None, n1, t1, D_BRANCH), lambda b, i: (ZC_IN, b, 0, i, 0)),
                  pl.BlockSpec((t1 // _dft_group(n1), DFT_K, DFT_K), lambda b, i: (i, 0, 0)),
                  full, full],
        out_specs=pl.BlockSpec((None, 2 * n1, t1, D_BRANCH), lambda b, i: (b, 0, i, 0)),
        out_shape=jax.ShapeDtypeStruct((B, 2 * n1, n2, D_BRANCH), BF16),
        compiler_params=_params(2), name="fnet_stage1",
    )(z.reshape(nz, B, n1, n2, D_BRANCH), gg, mc, ms)

    t2 = min(n1, DFT_TOKENS * 2 // n2)
    nk = n1 // t2
    y = pl.pallas_call(
        functools.partial(_fnet2_kernel, norm=float(1.0 / np.sqrt(L * D_GROUP))),
        grid=(B, nk),
        in_specs=[pl.BlockSpec((None, t2, n2, D_BRANCH), lambda b, i: (b, i, 0, 0)),
                  pl.BlockSpec((None, t2, n2, D_BRANCH), lambda b, i: (b, nk + i, 0, 0)),
                  pl.BlockSpec((n2, 2 * n2), lambda b, i: (0, 0))],
        out_specs=pl.BlockSpec((None, n2, t2, D_BRANCH), lambda b, i: (b, 0, i, 0)),
        out_shape=jax.ShapeDtypeStruct((B, n2, n1, D_BRANCH), BF16),
        compiler_params=_params(2), name="fnet_stage2",
    )(a, a, cs2)
    return y.reshape(B, L, D_BRANCH)


def _na_key_col0(cb):
    return int(np.clip(cb * NA_CQ - NA_KW // 2, 0, GRID_W - NA_CK))


@functools.lru_cache(maxsize=None)
def _na_tables():
    i = np.arange(NA_TR)[:, None]
    j = np.arange(NA_KROWS)[None, :]
    row_sel = np.zeros((3, NA_TR, NA_KROWS, 2 * NA_KH - 1), np.float32)
    row_ok = np.zeros((3, NA_TR, NA_KROWS), bool)
    for var, (base, rel) in enumerate(((0, np.zeros(NA_TR, int)), (-NA_TR, np.arange(NA_TR)),
                                       (-NA_KH, np.full(NA_TR, NA_KROWS - NA_KH)))):
        ok = (j >= rel[:, None]) & (j < rel[:, None] + NA_KH)
        dr = base + j - i + NA_KH - 1
        for ii, jj in zip(*np.nonzero(ok)):
            row_sel[var, ii, jj, dr[ii, jj]] = 1.0
        row_ok[var] = ok
    col_sel = np.zeros((NA_NCB, NA_CQ, NA_CK, 2 * NA_KW - 1), np.float32)
    col_ok = np.zeros((NA_NCB, NA_CQ, NA_CK), bool)
    for cb in range(NA_NCB):
        for cq in range(NA_CQ):
            c = cb * NA_CQ + cq
            cst = int(np.clip(c - NA_KW // 2, 0, GRID_W - NA_KW))
            for kl in range(NA_CK):
                kc = _na_key_col0(cb) + kl
                if cst <= kc < cst + NA_KW:
                    col_sel[cb, cq, kl, kc - c + NA_KW - 1] = 1.0
                    col_ok[cb, cq, kl] = True
    ok = row_ok[:, None, :, None, :, None] & col_ok[None, :, None, :, None, :]
    mask = np.where(ok, 0.0, NEG_BIG).astype(np.float32).reshape(3, NA_NCB, 1, NA_BQ, NA_BK)
    nb = 2 * NA_KW - 1
    col_sel_g = np.zeros((NA_JG, nb, NA_NCB, NA_CQ, NA_JG, NA_CK), np.float32)
    for j in range(NA_JG):
        col_sel_g[j, :, :, :, j, :] = np.transpose(col_sel, (3, 0, 1, 2))
    return row_sel, col_sel_g.reshape(NA_JG * nb, NA_NCB * NA_CQ * NA_JG * NA_CK), mask


def _na_bias(na_rpb):
    row_sel, col_sel_g, mask = _na_tables()
    hi = lax.Precision.HIGHEST
    ng = NA_KROWS // NA_JG
    b = jnp.einsum('vija,lhab->lvhijb', row_sel, na_rpb, precision=hi)
    b = b.reshape(DEPTH * 3 * N_GROUPS * NA_TR, ng, col_sel_g.shape[0])
    b = jnp.einsum('rgx,xn->rgn', b, col_sel_g, precision=hi)
    b = b.reshape(DEPTH, 3, N_GROUPS, NA_TR, ng, NA_NCB, NA_CQ, NA_JG * NA_CK)
    b = jnp.transpose(b, (0, 1, 5, 2, 3, 6, 4, 7))
    b = b.reshape(DEPTH, 3, NA_NCB, N_GROUPS, NA_BQ, NA_BK) + mask
    return b.reshape(DEPTH, 3, NA_NCB, N_GROUPS * NA_BQ, NA_BK)


def _na_kernel(*refs):
    q_ref, gate_ref = refs[:2]
    o_ref, acc_ref = refs[-2:]
    for t in range(NA_SUB):
        tile = refs[2 + 7 * t:9 + 7 * t]
        _na_tile(q_ref, t * NA_TQ, tile[0:3], tile[3:6], tile[6], acc_ref)
    o_ref[...] = (acc_ref[...] * _silu(gate_ref[...].astype(F32))).astype(BF16)


def _na_tile(q_ref, row0, k_refs, v_refs, bias_ref, acc_ref):
    head = lax.broadcasted_iota(jnp.int32, (NA_BQ, D_BRANCH), 1) // D_GROUP
    k = jnp.concatenate([r[...] for r in k_refs], axis=0).astype(F32)
    v = jnp.concatenate([r[...] for r in v_refs], axis=0).astype(F32)
    for cb in range(NA_NCB):
        c0 = row0 + cb * NA_CQ
        kc0 = _na_key_col0(cb)
        q = jnp.concatenate([q_ref[i * GRID_W + c0:i * GRID_W + c0 + NA_CQ, :]
                             for i in range(NA_TR)], axis=0)
        q = q * jnp.asarray(D_GROUP ** -0.5, BF16)
        zero = jnp.zeros_like(q)
        qs = jnp.concatenate([jnp.where(head == h, q, zero) for h in range(N_GROUPS)], axis=0)
        kb = jnp.concatenate([k[j * GRID_W + kc0:j * GRID_W + kc0 + NA_CK]
                              for j in range(NA_KROWS)], axis=0).astype(BF16)
        vb = jnp.concatenate([v[j * GRID_W + kc0:j * GRID_W + kc0 + NA_CK]
                              for j in range(NA_KROWS)], axis=0).astype(BF16)
        s = lax.dot_general(qs, kb, (((1,), (1,)), ((), ())), preferred_element_type=F32)
        s = s + bias_ref[cb]
        m = jnp.max(s, axis=-1, keepdims=True)
        p = jnp.exp(s - m)
        denom = jnp.sum(p, axis=-1, keepdims=True)
        o = jnp.dot(p.astype(BF16), vb, preferred_element_type=F32) / denom
        out = o[0:NA_BQ]
        for h in range(1, N_GROUPS):
            out = jnp.where(head == h, o[h * NA_BQ:(h + 1) * NA_BQ], out)
        for i in range(NA_TR):
            acc_ref[i * GRID_W + c0:i * GRID_W + c0 + NA_CQ, :] = out[i * NA_CQ:(i + 1) * NA_CQ]


def _na(z, bias):
    _, B, L, _ = z.shape
    nt = L // NA_TQ
    nkb = NA_KROWS // NA_TR
    assert nt % NA_SUB == 0 and nt >= nkb, L

    def kv(s, t, d):
        return pl.BlockSpec(
            (None, None, NA_TQ, D_BRANCH),
            lambda b, i: (s, b, jnp.clip(i * NA_SUB + t - 1, 0, nt - nkb) + d, 0))

    tile_specs, tile_args = [], []
    for t in range(NA_SUB):
        tile_specs += [kv(ZD_K, t, d) for d in range(nkb)] + [kv(ZD_V, t, d) for d in range(nkb)]
        tile_specs.append(pl.BlockSpec(
            (None, NA_NCB, N_GROUPS * NA_BQ, NA_BK),
            lambda b, i, t=t: (_tile_variant(i * NA_SUB + t, nt), 0, 0, 0)))
        tile_args += [z] * (2 * nkb) + [bias]
    return pl.pallas_call(
        _na_kernel,
        grid=(B, nt // NA_SUB),
        in_specs=[_slab(ZD_Q, NA_SUB * NA_TQ), _slab(ZD_GATE, NA_SUB * NA_TQ)] + tile_specs,
        out_specs=_rows(NA_SUB * NA_TQ, D_BRANCH),
        out_shape=jax.ShapeDtypeStruct((B, L, D_BRANCH), BF16),
        scratch_shapes=[pltpu.VMEM((NA_SUB * NA_TQ, D_BRANCH), F32)],
        compiler_params=_params(2), name="na",
    )(z, z, *tile_args)


def _final_kernel(ya_ref, yb_ref, yc_ref, yd_ref, cg_ref, x_ref, gate_ref, w_ref, fg_ref, o_ref):
    y = _mix_out([ya_ref[...], yb_ref[...], _gated(yc_ref[...], cg_ref[...]), yd_ref[...]], w_ref)
    o_ref[...] = _rms(x_ref[...] + gate_ref[...] * y) * fg_ref[...]


def _final(ys, z_prev, x, gate, w_bf, final_g, tm=TM):
    B, L, _ = x.shape
    return pl.pallas_call(
        _final_kernel,
        grid=(B, L // tm),
        in_specs=[_rows(tm, D_BRANCH)] * 4 + [
            _slab(ZC_GATE, tm),
            _rows(tm, D_MODEL), pl.BlockSpec((None, 1, D_MODEL), lambda b, i: (b, 0, 0)),
            _const((D_MODEL, D_MODEL)), _const((1, D_MODEL))],
        out_specs=_rows(tm, D_MODEL),
        out_shape=jax.ShapeDtypeStruct((B, L, D_MODEL), F32),
        compiler_params=_params(2), name="final",
    )(*ys, z_prev, x, gate, w_bf, final_g.reshape(1, D_MODEL))


def _trunk(x, mod, prep, norm_g, pool_scale, sgu_norm_g, final_norm_g):
    B = x.shape[0]
    ys, z, gate = None, None, None
    for l in range(DEPTH):
        shift, scl, gate_l = (mod[l, :, n * D_MODEL:(n + 1) * D_MODEL].reshape(B, 1, D_MODEL)
                              for n in range(3))
        x, z, ya, yb = _layer(ys, z, x, gate, prep["w_out"][l - 1], norm_g[l], shift, scl,
                              prep["w_in"][l], sgu_norm_g[l], prep["sgu_w"][l], prep["sgu_b"][l],
                              prep["pool_w"][l], pool_scale[l])
        yc = _fnet(z, prep["fnet_mc"][l], prep["fnet_ms"][l])
        yd = _na(z, prep["na_bias"][l])
        ys, gate = (ya, yb, yc, yd), gate_l
    return _final(ys, z, x, gate, prep["w_out"][DEPTH - 1], final_norm_g)


def kernel(x_prompt, x_sample, c_prompt, c_sample, norm_g, w_ada, b_ada, w_in, w_out, pool_w, pool_scale, sgu_norm_g, sgu_w, sgu_b, fnet_w, na_rpb, final_norm_g):
    nb_p, nb_s = c_prompt.shape[0], c_sample.shape[0]
    c_all = jnp.concatenate(
        [c_prompt, c_sample, jnp.zeros((C_PAD - nb_p - nb_s, D_MODEL), F32)], axis=0)
    mod = _adaln(c_all, w_ada, b_ada)

    mc, ms = _fnet_fold(_block_diag(fnet_w))
    prep = {
        "w_in": w_in.astype(BF16),
        "w_out": w_out.astype(BF16),
        "pool_w": _block_diag(pool_w).astype(BF16),
        "sgu_w": sgu_w.astype(BF16),
        "sgu_b": jnp.repeat(jnp.swapaxes(sgu_b, 1, 2), D_GROUP, axis=2),
        "fnet_mc": mc,
        "fnet_ms": ms,
        "na_bias": _na_bias(na_rpb),
    }
    y_prompt = _trunk(x_prompt, mod[:, :nb_p], prep, norm_g, pool_scale, sgu_norm_g, final_norm_g)
    y_sample = _trunk(x_sample, mod[:, nb_p:nb_p + nb_s], prep, norm_g, pool_scale, sgu_norm_g,
                      final_norm_g)
    return (y_prompt, y_sample)
```

```python
import functools

import numpy as np
import jax
import jax.numpy as jnp
from jax import lax
from jax.experimental import pallas as pl
from jax.experimental.pallas import tpu as pltpu

F32 = jnp.float32
BF16 = jnp.bfloat16
BF16_ROWS = 16

D_MODEL = 1024
DEPTH = 4
D_BRANCH = 256
N_GROUPS = 4
D_GROUP = 64
POOL_WINDOWS = (2, 4, 8, 16)
POOL_HALO = BF16_ROWS
CHUNK = 128
GRID_W = 64
NA_KH = 8
NA_KW = 16
N_IN_SLICES = 11
D_IN = N_IN_SLICES * D_BRANCH
RMS_EPS = 1e-6
LN_EPS = 1e-5
NEG_BIG = -1e30

A_IN, A_GATE, B_U, B_V, B_GATE, C_IN, C_GATE, D_Q, D_K, D_V, D_GATE = range(N_IN_SLICES)
Z_SLICES = (C_IN, C_GATE, D_Q, D_K, D_V, D_GATE)
ZC_IN, ZC_GATE, ZD_Q, ZD_K, ZD_V, ZD_GATE = range(len(Z_SLICES))

NA_TR = 4
NA_KROWS = NA_TR + NA_KH
NA_TQ = NA_TR * GRID_W
NA_SUB = 8
NA_CQ = 16
NA_NCB = GRID_W // NA_CQ
NA_CK = 2 * NA_KW
NA_BQ = NA_TR * NA_CQ
NA_BK = NA_KROWS * NA_CK
NA_JG = 128 // NA_CK
C_PAD = 16
DFT_T = BF16_ROWS
DFT_K = 256
DFT_TOKENS = 2048
TM = 1024
LAYER_ROWS = 256

VMEM_LIMIT = 56 * 1024 * 1024


def _params(n_axes, vmem=VMEM_LIMIT):
    return pltpu.CompilerParams(dimension_semantics=("arbitrary",) * n_axes,
                                vmem_limit_bytes=vmem)


def _silu(x):
    return x / (1.0 + jnp.exp(-x))


def _gelu(x):
    return x * (0.5 * (1.0 + jnp.tanh(np.sqrt(2.0 / np.pi) * (x + 0.044715 * (x * x * x)))))


def _rms(x):
    return x * lax.rsqrt(jnp.mean(x * x, axis=-1, keepdims=True) + RMS_EPS)


def _tile_variant(i, n):
    return jnp.where(i == 0, 0, jnp.where(i == n - 1, 2, 1))


def _const(shape):
    zeros = (0,) * len(shape)
    return pl.BlockSpec(shape, lambda b, i: zeros, pipeline_mode=pl.Buffered(1))


def _rows(t, width):
    return pl.BlockSpec((None, t, width), lambda b, i: (b, i, 0))


def _slab(s, t):
    return pl.BlockSpec((None, None, t, D_BRANCH), lambda b, i: (s, b, i, 0))


def _adaln_kernel(c_ref, w_ref, b_ref, o_ref):
    s = _silu(c_ref[...])
    o_ref[...] = jnp.dot(s, w_ref[...], precision=lax.Precision.HIGHEST,
                         preferred_element_type=F32) + b_ref[...]


def _adaln(c_all, w_ada, b_ada):
    return pl.pallas_call(
        _adaln_kernel,
        grid=(DEPTH, 3),
        in_specs=[pl.BlockSpec((C_PAD, D_MODEL), lambda l, n: (0, 0)),
                  pl.BlockSpec((None, D_MODEL, D_MODEL), lambda l, n: (l, 0, n)),
                  pl.BlockSpec((None, 1, D_MODEL), lambda l, n: (l, 0, n))],
        out_specs=pl.BlockSpec((None, C_PAD, D_MODEL), lambda l, n: (l, 0, n)),
        out_shape=jax.ShapeDtypeStruct((DEPTH, C_PAD, 3 * D_MODEL), F32),
        compiler_params=_params(2), name="adaln",
    )(c_all, w_ada, b_ada.reshape(DEPTH, 1, 3 * D_MODEL))


@functools.lru_cache(maxsize=None)
def _pool_tables(tp):
    r = np.arange(CHUNK)[:, None]
    c = np.arange(CHUNK + 2 * POOL_HALO)[None, :]
    band = np.stack([(c >= r + POOL_HALO - w // 2) & (c < r + POOL_HALO + w // 2)
                     for w in POOL_WINDOWS]).astype(np.float32)
    t = np.arange(tp)[:, None]
    w = np.repeat(np.array(POOL_WINDOWS), D_GROUP)[None, :]
    head = np.minimum(t + w // 2, tp + w) - np.maximum(t - w // 2, 0)
    tail = np.minimum(t + w // 2, tp) - np.maximum(t - w // 2, -w)
    inv = np.stack([1.0 / head, 1.0 / np.broadcast_to(w, head.shape), 1.0 / tail])
    return band, inv.astype(np.float32)


def _gated(y, gate):
    return (y.astype(F32) * _silu(gate.astype(F32))).astype(BF16)


def _mix_out(ys, w_ref):
    y = jnp.dot(ys[0], w_ref[0:D_BRANCH, :], preferred_element_type=F32)
    for n in range(1, len(ys)):
        y = y + jnp.dot(ys[n], w_ref[n * D_BRANCH:(n + 1) * D_BRANCH, :],
                        preferred_element_type=F32)
    return y


def _modulate(x, g_ref, shift_ref, scl_ref):
    return ((_rms(x) * g_ref[...]) * (1.0 + scl_ref[...]) + shift_ref[...]).astype(BF16)


def _project(h, w_ref, s):
    return jnp.dot(h, w_ref[:, s * D_BRANCH:(s + 1) * D_BRANCH], preferred_element_type=F32)


def _sgu_body(u, v, gate, g_ref, w_ref, b_ref, o_ref):
    t = u.shape[0]
    u = _gelu(u)
    v = _gelu(v)
    mu = jnp.mean(v, axis=-1, keepdims=True)
    d = v - mu
    var = jnp.mean(d * d, axis=-1, keepdims=True)
    vn = (d * lax.rsqrt(var + LN_EPS) * g_ref[...]).astype(BF16)
    gate = _silu(gate)
    head = lax.broadcasted_iota(jnp.int32, (CHUNK, D_BRANCH), 1) // D_GROUP
    zero = jnp.zeros((CHUNK, D_BRANCH), BF16)
    for n in range(t // CHUNK):
        rows = slice(n * CHUNK, (n + 1) * CHUNK)
        vc = vn[rows]
        s = None
        for p in range(N_GROUPS // 2):
            rhs = jnp.concatenate([jnp.where(head == 2 * p, vc, zero),
                                   jnp.where(head == 2 * p + 1, vc, zero)], axis=0)
            sp = jnp.dot(w_ref[p], rhs, preferred_element_type=F32)
            s = sp if s is None else s + sp
        o_ref[rows, :] = ((u[rows] * (s + b_ref[...])) * gate[rows]).astype(BF16)


def _pool_body(ext_ref, gate, band_ref, inv_ref, w_ref, scale_ref, o_ref):
    t = gate.shape[0]
    H = POOL_HALO
    gate = _silu(gate)
    group = lax.broadcasted_iota(jnp.int32, (CHUNK, D_BRANCH), 1) // D_GROUP
    for m in range(t // CHUNK):
        rows = slice(m * CHUNK, (m + 1) * CHUNK)
        e = ext_ref[m * CHUNK:(m + 1) * CHUNK + 2 * H, :]
        s = jnp.dot(band_ref[0], e, preferred_element_type=F32)
        for g in range(1, N_GROUPS):
            s = jnp.where(group == g, jnp.dot(band_ref[g], e, preferred_element_type=F32), s)
        p = s * inv_ref[rows, :] - e[H:H + CHUNK].astype(F32)
        y = jnp.dot(p.astype(BF16), w_ref[...], preferred_element_type=F32) * scale_ref[...]
        o_ref[rows, :] = (y * gate[rows]).astype(BF16)


def _layer_kernel(*refs, tm, first):
    n_stream = 2 if first else 14
    stream, rest = refs[:n_stream], refs[n_stream:]
    (g_ref, shift_ref, scl_ref, win_ref, sg_ref, sw_ref, sb_ref,
     band_ref, inv_ref, pw_ref, ps_ref) = rest[:11]
    outs = rest[11:]
    i = pl.program_id(1)
    n = pl.num_programs(1)
    H = POOL_HALO
    ext_ref, carry_ref = outs[-2:]
    slot_prev = (i + 1) % 2
    slot_this = i % 2

    @pl.when(i == 0)
    def _():
        carry_ref[slot_prev] = jnp.zeros((H, D_BRANCH), BF16)

    if first:
        x_ref, xh_ref = stream
        z_ref, ya_ref, yb_ref = outs[:3]
        h_ext = _modulate(jnp.concatenate([x_ref[...], xh_ref[...]], axis=0),
                          g_ref, shift_ref, scl_ref)
    else:
        ys, x_ref, ysh, xh_ref = stream[0:4], stream[4], stream[5:9], stream[9]
        cg_ref, cgh_ref, gprev_ref, wout_ref = stream[10:14]
        xo_ref, z_ref, ya_ref, yb_ref = outs[:4]
        ys_ext = [jnp.concatenate([m[...], mh[...]], axis=0) for m, mh in zip(ys, ysh)]
        ys_ext[2] = _gated(ys_ext[2], jnp.concatenate([cg_ref[...], cgh_ref[...]], axis=0))
        h_parts = []
        for r0 in range(0, tm, LAYER_ROWS):
            r1 = r0 + LAYER_ROWS
            if r1 == tm:
                r1 = tm + H
                x_in = jnp.concatenate([x_ref[r0:tm, :], xh_ref[...]], axis=0)
            else:
                x_in = x_ref[r0:r1, :]
            x = x_in + gprev_ref[...] * _mix_out([m[r0:r1] for m in ys_ext], wout_ref)
            xo_ref[r0:min(r1, tm), :] = x[0:min(r1, tm) - r0]
            h_parts.append(_modulate(x, g_ref, shift_ref, scl_ref))
        h_ext = jnp.concatenate(h_parts, axis=0)

    h = h_ext[0:tm]
    zl = jnp.dot(h_ext, win_ref[:, 0:Z_SLICES[0] * D_BRANCH], preferred_element_type=F32)

    def local(s):
        return zl[0:tm, s * D_BRANCH:(s + 1) * D_BRANCH]

    _sgu_body(local(B_U), local(B_V), local(B_GATE), sg_ref, sw_ref, sb_ref, yb_ref)

    a_ext = zl[:, A_IN * D_BRANCH:(A_IN + 1) * D_BRANCH].astype(BF16)
    a = a_ext[0:tm]
    a_next = a_ext[tm:tm + H]
    ext_ref[0:H, :] = carry_ref[slot_prev]
    ext_ref[H:H + tm, :] = a
    ext_ref[H + tm:2 * H + tm, :] = jnp.where(i < n - 1, a_next, jnp.zeros_like(a_next))
    carry_ref[slot_this] = a[tm - H:tm]
    _pool_body(ext_ref, local(A_GATE), band_ref, inv_ref, pw_ref, ps_ref, ya_ref)

    zg = jnp.dot(h, win_ref[:, Z_SLICES[0] * D_BRANCH:], preferred_element_type=F32)
    for k in range(len(Z_SLICES)):
        z_ref[k] = zg[:, k * D_BRANCH:(k + 1) * D_BRANCH].astype(BF16)


def _of_layer(l, shape):
    zeros = (0,) * len(shape)
    return pl.BlockSpec((None, *shape), lambda b, i: (l, *zeros), pipeline_mode=pl.Buffered(1))


def _mod_vec(l, n, b0):
    return pl.BlockSpec((None, None, None, 1, D_MODEL), lambda b, i: (l, n, b0 + b, 0, 0))


def _layer(l, b0, ys, z_prev, x, mod, prm, tm=TM):
    B, L, _ = x.shape
    first = ys is None
    H = POOL_HALO
    nt = L // tm
    nh = L // H
    band, inv = _pool_tables(tm)

    def halo(width):
        return pl.BlockSpec((None, H, width),
                            lambda b, i: (b, jnp.minimum((i + 1) * (tm // H), nh - 1), 0))

    if first:
        stream_specs = [_rows(tm, D_MODEL), halo(D_MODEL)]
        stream_args = [x, x]
    else:
        gate_halo = pl.BlockSpec(
            (None, None, H, D_BRANCH),
            lambda b, i: (ZC_GATE, b, jnp.minimum((i + 1) * (tm // H), nh - 1), 0))
        stream_specs = ([_rows(tm, D_BRANCH)] * 4 + [_rows(tm, D_MODEL)] + [halo(D_BRANCH)] * 4
                        + [halo(D_MODEL), _slab(ZC_GATE, tm), gate_halo, _mod_vec(l - 1, 2, b0),
                           _of_layer(l - 1, (D_MODEL, D_MODEL))])
        stream_args = [*ys, x, *ys, x, z_prev, z_prev, mod, prm["w_out"]]
    param_specs = [_of_layer(l, (1, D_MODEL)), _mod_vec(l, 0, b0), _mod_vec(l, 1, b0),
                   _of_layer(l, (D_MODEL, D_IN)),
                   _of_layer(l, (1, D_BRANCH)), _of_layer(l, (N_GROUPS // 2, CHUNK, 2 * CHUNK)),
                   _of_layer(l, (CHUNK, D_BRANCH)),
                   _const(band.shape),
                   pl.BlockSpec((None, tm, D_BRANCH), lambda b, i: (_tile_variant(i, nt), 0, 0)),
                   _of_layer(l, (D_BRANCH, D_BRANCH)), _of_layer(l, (1, D_BRANCH))]
    param_args = [prm["norm_g"], mod, mod, prm["w_in"],
                  prm["sgu_g"], prm["sgu_w"], prm["sgu_b"],
                  jnp.asarray(band).astype(BF16), jnp.asarray(inv), prm["pool_w"],
                  prm["pool_scale"]]
    nz = len(Z_SLICES)
    out_specs = [pl.BlockSpec((nz, None, tm, D_BRANCH), lambda b, i: (0, b, i, 0)),
                 _rows(tm, D_BRANCH), _rows(tm, D_BRANCH)]
    out_shape = [jax.ShapeDtypeStruct((nz, B, L, D_BRANCH), BF16),
                 jax.ShapeDtypeStruct((B, L, D_BRANCH), BF16),
                 jax.ShapeDtypeStruct((B, L, D_BRANCH), BF16)]
    if not first:
        out_specs = [_rows(tm, D_MODEL)] + out_specs
        out_shape = [jax.ShapeDtypeStruct((B, L, D_MODEL), F32)] + out_shape
    res = pl.pallas_call(
        functools.partial(_layer_kernel, tm=tm, first=first),
        grid=(B, nt),
        in_specs=stream_specs + param_specs,
        out_specs=out_specs,
        out_shape=out_shape,
        scratch_shapes=[pltpu.VMEM((tm + 2 * H, D_BRANCH), BF16),
                        pltpu.VMEM((2, H, D_BRANCH), BF16)],
        compiler_params=_params(2), name="layer_first" if first else "layer",
    )(*stream_args, *param_args)
    return (x, *res) if first else tuple(res)


def _dft_split(L):
    n2 = 128
    return L // n2, n2


def _dft_group(n1):
    return DFT_K // (2 * n1)


@functools.lru_cache(maxsize=None)
def _fnet_tables(L):
    n1, n2 = _dft_split(L)
    k1 = np.arange(n1, dtype=np.int64)
    n = np.arange(n2, dtype=np.int64)[:, None, None] + n2 * np.arange(n1, dtype=np.int64)[None, None, :]
    ang = 2.0 * np.pi * ((k1[None, :, None] * n) % L).astype(np.float64) / L
    gc, gs = np.cos(ang), np.sin(ang)
    gg = np.concatenate([np.concatenate([gc, -gs], axis=2),
                         np.concatenate([-gs, -gc], axis=2)], axis=1)
    jb = _dft_group(n1)
    grouped = np.zeros((n2 // jb, jb, 2 * n1, jb, 2 * n1))
    for j in range(jb):
        grouped[:, j, :, j, :] = gg[j::jb]
    gg = grouped.reshape(n2 // jb, DFT_K, DFT_K)
    kk = np.arange(n2, dtype=np.int64)
    ang2 = 2.0 * np.pi * ((kk[:, None] * kk[None, :]) % n2).astype(np.float64) / n2
    cs2 = np.concatenate([np.cos(ang2), np.sin(ang2)], axis=1)
    return gg.astype(np.float32), cs2.astype(np.float32)


@functools.lru_cache(maxsize=None)
def _channel_tables():
    c = np.arange(D_GROUP, dtype=np.int64)
    ang = 2.0 * np.pi * ((c[:, None] * c[None, :]) % D_GROUP).astype(np.float64) / D_GROUP
    eye = np.eye(N_GROUPS)
    return (np.kron(eye, np.cos(ang)).astype(np.float32),
            np.kron(eye, np.sin(ang)).astype(np.float32))


def _block_diag(w):
    eye = jnp.eye(N_GROUPS, dtype=w.dtype)
    out = jnp.einsum('...gcd,gh->...gchd', w, eye)
    return out.reshape(*w.shape[:-3], D_BRANCH, D_BRANCH)


def _fold_kernel(bdc_ref, bds_ref, w_ref, mc_ref, ms_ref):
    w = w_ref[...]
    hi = lax.Precision.HIGHEST
    mc_ref[...] = jnp.dot(bdc_ref[...], w, precision=hi, preferred_element_type=F32).astype(BF16)
    ms_ref[...] = jnp.dot(bds_ref[...], w, precision=hi, preferred_element_type=F32).astype(BF16)


def _fnet_fold(fnet_w_bd):
    bdc, bds = _channel_tables()
    full = pl.BlockSpec((D_BRANCH, D_BRANCH), lambda l: (0, 0))
    per = pl.BlockSpec((None, D_BRANCH, D_BRANCH), lambda l: (l, 0, 0))
    shp = jax.ShapeDtypeStruct((DEPTH, D_BRANCH, D_BRANCH), BF16)
    return pl.pallas_call(
        _fold_kernel, grid=(DEPTH,), in_specs=[full, full, per], out_specs=[per, per],
        out_shape=[shp, shp], compiler_params=_params(1), name="fnet_fold",
    )(jnp.asarray(bdc), jnp.asarray(bds), fnet_w_bd)


def _fnet1_kernel(x_ref, gg_ref, mc_ref, ms_ref, a_ref, *, n1, t):
    jb = _dft_group(n1)
    xt = jnp.swapaxes(x_ref[...].astype(F32), 0, 1)
    x2 = xt.reshape(t * n1, D_BRANCH).astype(BF16)
    u = jnp.dot(x2, mc_ref[...], preferred_element_type=F32)
    v = jnp.dot(x2, ms_ref[...], preferred_element_type=F32)
    groups = []
    for g in range(t // jb):
        parts = []
        for j in range(g * jb, (g + 1) * jb):
            parts += [u[j * n1:(j + 1) * n1], v[j * n1:(j + 1) * n1]]
        uv = jnp.concatenate(parts, axis=0).astype(BF16)
        a = jnp.dot(gg_ref[g], uv, preferred_element_type=F32)
        groups.append(a.reshape(jb, 2 * n1, D_BRANCH))
    a_ref[...] = jnp.swapaxes(jnp.concatenate(groups, axis=0), 0, 1).astype(BF16)


def _fnet2_kernel(ar_ref, ai_ref, cs_ref, o_ref, *, norm):
    ys = []
    for j in range(ar_ref.shape[0]):
        a = jnp.concatenate([ar_ref[j], ai_ref[j]], axis=0)
        ys.append(jnp.dot(cs_ref[...], a, preferred_element_type=F32))
    y = jnp.swapaxes(jnp.stack(ys, axis=0), 0, 1)
    o_ref[...] = (y * norm).astype(BF16)


def _fnet(l, z, mc, ms):
    nz, B, L, _ = z.shape
    n1, n2 = _dft_split(L)
    T = DFT_T
    assert L == n1 * n2 and n1 % T == 0 and DFT_K % (2 * n1) == 0, L
    gg, cs2 = _fnet_tables(L)
    gg, cs2 = jnp.asarray(gg).astype(BF16), jnp.asarray(cs2).astype(BF16)
    full = _of_layer(l, (D_BRANCH, D_BRANCH))
    t1 = min(n2, max(T, DFT_TOKENS // n1))
    a = pl.pallas_call(
        functools.partial(_fnet1_kernel, n1=n1, t=t1),
        grid=(B, n2 // t1),
        in_specs=[pl.BlockSpec((None, None, n1, t1, D_BRANCH), lambda b, i: (ZC_IN, b, 0, i, 0)),
                  pl.BlockSpec((t1 // _dft_group(n1), DFT_K, DFT_K), lambda b, i: (i, 0, 0)),
                  full, full],
        out_specs=pl.BlockSpec((None, 2 * n1, t1, D_BRANCH), lambda b, i: (b, 0, i, 0)),
        out_shape=jax.ShapeDtypeStruct((B, 2 * n1, n2, D_BRANCH), BF16),
        compiler_params=_params(2), name="fnet_stage1",
    )(z.reshape(nz, B, n1, n2, D_BRANCH), gg, mc, ms)

    t2 = min(n1, DFT_TOKENS * 2 // n2)
    nk = n1 // t2
    y = pl.pallas_call(
        functools.partial(_fnet2_kernel, norm=float(1.0 / np.sqrt(L * D_GROUP))),
        grid=(B, nk),
        in_specs=[pl.BlockSpec((None, t2, n2, D_BRANCH), lambda b, i: (b, i, 0, 0)),
                  pl.BlockSpec((None, t2, n2, D_BRANCH), lambda b, i: (b, nk + i, 0, 0)),
                  pl.BlockSpec((n2, 2 * n2), lambda b, i: (0, 0))],
        out_specs=pl.BlockSpec((None, n2, t2, D_BRANCH), lambda b, i: (b, 0, i, 0)),
        out_shape=jax.ShapeDtypeStruct((B, n2, n1, D_BRANCH), BF16),
        compiler_params=_params(2), name="fnet_stage2",
    )(a, a, cs2)
    return y.reshape(B, L, D_BRANCH)


def _na_key_col0(cb):
    return int(np.clip(cb * NA_CQ - NA_KW // 2, 0, GRID_W - NA_CK))


@functools.lru_cache(maxsize=None)
def _na_tables():
    i = np.arange(NA_TR)[:, None]
    j = np.arange(NA_KROWS)[None, :]
    row_sel = np.zeros((3, NA_TR, NA_KROWS, 2 * NA_KH - 1), np.float32)
    row_ok = np.zeros((3, NA_TR, NA_KROWS), bool)
    for var, (base, rel) in enumerate(((0, np.zeros(NA_TR, int)), (-NA_TR, np.arange(NA_TR)),
                                       (-NA_KH, np.full(NA_TR, NA_KROWS - NA_KH)))):
        ok = (j >= rel[:, None]) & (j < rel[:, None] + NA_KH)
        dr = base + j - i + NA_KH - 1
        for ii, jj in zip(*np.nonzero(ok)):
            row_sel[var, ii, jj, dr[ii, jj]] = 1.0
        row_ok[var] = ok
    col_sel = np.zeros((NA_NCB, NA_CQ, NA_CK, 2 * NA_KW - 1), np.float32)
    col_ok = np.zeros((NA_NCB, NA_CQ, NA_CK), bool)
    for cb in range(NA_NCB):
        for cq in range(NA_CQ):
            c = cb * NA_CQ + cq
            cst = int(np.clip(c - NA_KW // 2, 0, GRID_W - NA_KW))
            for kl in range(NA_CK):
                kc = _na_key_col0(cb) + kl
                if cst <= kc < cst + NA_KW:
                    col_sel[cb, cq, kl, kc - c + NA_KW - 1] = 1.0
                    col_ok[cb, cq, kl] = True
    ok = row_ok[:, None, :, None, :, None] & col_ok[None, :, None, :, None, :]
    mask = np.where(ok, 0.0, NEG_BIG).astype(np.float32).reshape(3, NA_NCB, 1, NA_BQ, NA_BK)
    nb = 2 * NA_KW - 1
    col_sel_g = np.zeros((NA_JG, nb, NA_NCB, NA_CQ, NA_JG, NA_CK), np.float32)
    for j in range(NA_JG):
        col_sel_g[j, :, :, :, j, :] = np.transpose(col_sel, (3, 0, 1, 2))
    return row_sel, col_sel_g.reshape(NA_JG * nb, NA_NCB * NA_CQ * NA_JG * NA_CK), mask


def _na_bias(na_rpb):
    row_sel, col_sel_g, mask = _na_tables()
    hi = lax.Precision.HIGHEST
    ng = NA_KROWS // NA_JG
    b = jnp.einsum('vija,lhab->lvhijb', row_sel, na_rpb, precision=hi)
    b = b.reshape(DEPTH * 3 * N_GROUPS * NA_TR, ng, col_sel_g.shape[0])
    b = jnp.einsum('rgx,xn->rgn', b, col_sel_g, precision=hi)
    b = b.reshape(DEPTH, 3, N_GROUPS, NA_TR, ng, NA_NCB, NA_CQ, NA_JG * NA_CK)
    b = jnp.transpose(b, (0, 1, 5, 2, 3, 6, 4, 7))
    b = b.reshape(DEPTH, 3, NA_NCB, N_GROUPS, NA_BQ, NA_BK) + mask
    return b.reshape(DEPTH, 3, NA_NCB, N_GROUPS * NA_BQ, NA_BK)


def _na_kernel(*refs):
    q_ref, gate_ref = refs[:2]
    o_ref, acc_ref = refs[-2:]
    for t in range(NA_SUB):
        bias_ref = refs[2 + (0 if t == 0 else 2 if t == NA_SUB - 1 else 1)]
        tile = refs[5 + 6 * t:11 + 6 * t]
        _na_tile(q_ref, t * NA_TQ, tile[0:3], tile[3:6], bias_ref, acc_ref)
    o_ref[...] = (acc_ref[...] * _silu(gate_ref[...].astype(F32))).astype(BF16)


def _na_tile(q_ref, row0, k_refs, v_refs, bias_ref, acc_ref):
    head = lax.broadcasted_iota(jnp.int32, (NA_BQ, D_BRANCH), 1) // D_GROUP
    k = jnp.concatenate([r[...] for r in k_refs], axis=0).astype(F32)
    v = jnp.concatenate([r[...] for r in v_refs], axis=0).astype(F32)
    for cb in range(NA_NCB):
        c0 = row0 + cb * NA_CQ
        kc0 = _na_key_col0(cb)
        q = jnp.concatenate([q_ref[i * GRID_W + c0:i * GRID_W + c0 + NA_CQ, :]
                             for i in range(NA_TR)], axis=0)
        q = q * jnp.asarray(D_GROUP ** -0.5, BF16)
        zero = jnp.zeros_like(q)
        qs = jnp.concatenate([jnp.where(head == h, q, zero) for h in range(N_GROUPS)], axis=0)
        kb = jnp.concatenate([k[j * GRID_W + kc0:j * GRID_W + kc0 + NA_CK]
                              for j in range(NA_KROWS)], axis=0).astype(BF16)
        vb = jnp.concatenate([v[j * GRID_W + kc0:j * GRID_W + kc0 + NA_CK]
                              for j in range(NA_KROWS)], axis=0).astype(BF16)
        s = lax.dot_general(qs, kb, (((1,), (1,)), ((), ())), preferred_element_type=F32)
        s = s + bias_ref[cb]
        m = jnp.max(s, axis=-1, keepdims=True)
        p = jnp.exp(s - m)
        denom = jnp.sum(p, axis=-1, keepdims=True)
        o = jnp.dot(p.astype(BF16), vb, preferred_element_type=F32) / denom
        out = o[0:NA_BQ]
        for h in range(1, N_GROUPS):
            out = jnp.where(head == h, o[h * NA_BQ:(h + 1) * NA_BQ], out)
        for i in range(NA_TR):
            acc_ref[i * GRID_W + c0:i * GRID_W + c0 + NA_CQ, :] = out[i * NA_CQ:(i + 1) * NA_CQ]


def _na(l, z, bias):
    _, B, L, _ = z.shape
    nt = L // NA_TQ
    nkb = NA_KROWS // NA_TR
    assert nt % NA_SUB == 0 and nt >= nkb, L

    def kv(s, t, d):
        return pl.BlockSpec(
            (None, None, NA_TQ, D_BRANCH),
            lambda b, i: (s, b, jnp.clip(i * NA_SUB + t - 1, 0, nt - nkb) + d, 0))

    def bias_of(t):
        return pl.BlockSpec(
            (None, None, NA_NCB, N_GROUPS * NA_BQ, NA_BK),
            lambda b, i: (l, _tile_variant(i * NA_SUB + t, nt), 0, 0, 0))

    bias_inner = pl.BlockSpec((None, None, NA_NCB, N_GROUPS * NA_BQ, NA_BK),
                              lambda b, i: (l, 1, 0, 0, 0), pipeline_mode=pl.Buffered(1))
    tile_specs = []
    for t in range(NA_SUB):
        tile_specs += [kv(ZD_K, t, d) for d in range(nkb)] + [kv(ZD_V, t, d) for d in range(nkb)]
    return pl.pallas_call(
        _na_kernel,
        grid=(B, nt // NA_SUB),
        in_specs=[_slab(ZD_Q, NA_SUB * NA_TQ), _slab(ZD_GATE, NA_SUB * NA_TQ),
                  bias_of(0), bias_inner, bias_of(NA_SUB - 1)] + tile_specs,
        out_specs=_rows(NA_SUB * NA_TQ, D_BRANCH),
        out_shape=jax.ShapeDtypeStruct((B, L, D_BRANCH), BF16),
        scratch_shapes=[pltpu.VMEM((NA_SUB * NA_TQ, D_BRANCH), F32)],
        compiler_params=_params(2), name="na",
    )(z, z, bias, bias, bias, *([z] * (2 * nkb * NA_SUB)))


def _final_kernel(ya_ref, yb_ref, yc_ref, yd_ref, cg_ref, x_ref, gate_ref, w_ref, fg_ref, o_ref):
    y = _mix_out([ya_ref[...], yb_ref[...], _gated(yc_ref[...], cg_ref[...]), yd_ref[...]], w_ref)
    o_ref[...] = _rms(x_ref[...] + gate_ref[...] * y) * fg_ref[...]


def _final(b0, ys, z_prev, x, mod, prm, final_g, tm=TM):
    B, L, _ = x.shape
    last = DEPTH - 1
    return pl.pallas_call(
        _final_kernel,
        grid=(B, L // tm),
        in_specs=[_rows(tm, D_BRANCH)] * 4 + [
            _slab(ZC_GATE, tm), _rows(tm, D_MODEL), _mod_vec(last, 2, b0),
            _of_layer(last, (D_MODEL, D_MODEL)), _const((1, D_MODEL))],
        out_specs=_rows(tm, D_MODEL),
        out_shape=jax.ShapeDtypeStruct((B, L, D_MODEL), F32),
        compiler_params=_params(2), name="final",
    )(*ys, z_prev, x, mod, prm["w_out"], final_g.reshape(1, D_MODEL))


def _trunk(x, b0, mod, prm, final_norm_g):
    ys, z = None, None
    for l in range(DEPTH):
        x, z, ya, yb = _layer(l, b0, ys, z, x, mod, prm)
        yc = _fnet(l, z, prm["fnet_mc"], prm["fnet_ms"])
        yd = _na(l, z, prm["na_bias"])
        ys = (ya, yb, yc, yd)
    return _final(b0, ys, z, x, mod, prm, final_norm_g)


def kernel(x_prompt, x_sample, c_prompt, c_sample, norm_g, w_ada, b_ada, w_in, w_out, pool_w, pool_scale, sgu_norm_g, sgu_w, sgu_b, fnet_w, na_rpb, final_norm_g):
    nb_p, nb_s = c_prompt.shape[0], c_sample.shape[0]
    c_all = jnp.concatenate(
        [c_prompt, c_sample, jnp.zeros((C_PAD - nb_p - nb_s, D_MODEL), F32)], axis=0)
    mod = _adaln(c_all, w_ada, b_ada)
    mod = jnp.transpose(mod.reshape(DEPTH, C_PAD, 3, 1, D_MODEL), (0, 2, 1, 3, 4))

    mc, ms = _fnet_fold(_block_diag(fnet_w))
    prm = {
        "norm_g": norm_g.reshape(DEPTH, 1, D_MODEL),
        "sgu_g": sgu_norm_g.reshape(DEPTH, 1, D_BRANCH),
        "pool_scale": pool_scale.reshape(DEPTH, 1, D_BRANCH),
        "w_in": w_in.astype(BF16),
        "w_out": w_out.astype(BF16),
        "pool_w": _block_diag(pool_w).astype(BF16),
        "sgu_w": jnp.swapaxes(sgu_w.reshape(DEPTH, N_GROUPS // 2, 2, CHUNK, CHUNK), 2, 3)
        .reshape(DEPTH, N_GROUPS // 2, CHUNK, 2 * CHUNK).astype(BF16),
        "sgu_b": jnp.repeat(jnp.swapaxes(sgu_b, 1, 2), D_GROUP, axis=2),
        "fnet_mc": mc,
        "fnet_ms": ms,
        "na_bias": _na_bias(na_rpb),
    }
    y_prompt = _trunk(x_prompt, 0, mod, prm, final_norm_g)
    y_sample = _trunk(x_sample, nb_p, mod, prm, final_norm_g)
    return (y_prompt, y_sample)
```

```python
import functools

import numpy as np
import jax
import jax.numpy as jnp
from jax import lax
from jax.experimental import pallas as pl
from jax.experimental.pallas import tpu as pltpu

F32 = jnp.float32
BF16 = jnp.bfloat16
BF16_ROWS = 16

D_MODEL = 1024
DEPTH = 4
D_BRANCH = 256
N_GROUPS = 4
D_GROUP = 64
POOL_WINDOWS = (2, 4, 8, 16)
POOL_HALO = BF16_ROWS
CHUNK = 128
GRID_W = 64
NA_KH = 8
NA_KW = 16
N_IN_SLICES = 11
D_IN = N_IN_SLICES * D_BRANCH
RMS_EPS = 1e-6
LN_EPS = 1e-5
NEG_BIG = -1e30

A_IN, A_GATE, B_U, B_V, B_GATE, C_IN, C_GATE, D_Q, D_K, D_V, D_GATE = range(N_IN_SLICES)
Z_SLICES = (C_IN, C_GATE, D_Q, D_K, D_V, D_GATE)
ZC_IN, ZC_GATE, ZD_Q, ZD_K, ZD_V, ZD_GATE = range(len(Z_SLICES))

NA_TR = 4
NA_KROWS = NA_TR + NA_KH
NA_TQ = NA_TR * GRID_W
NA_SUB = 8
NA_CQ = 16
NA_NCB = GRID_W // NA_CQ
NA_CK = 2 * NA_KW
NA_BQ = NA_TR * NA_CQ
NA_BK = NA_KROWS * NA_CK
NA_JG = 128 // NA_CK
C_PAD = 16
DFT_T = BF16_ROWS
DFT_K = 256
DFT_TOKENS = 2048
TM = 1024
LAYER_ROWS = 256

VMEM_LIMIT = 56 * 1024 * 1024


def _params(n_axes, vmem=VMEM_LIMIT):
    return pltpu.CompilerParams(dimension_semantics=("arbitrary",) * n_axes,
                                vmem_limit_bytes=vmem)


def _silu(x):
    return x / (1.0 + jnp.exp(-x))


def _gelu(x):
    return x * (0.5 * (1.0 + jnp.tanh(np.sqrt(2.0 / np.pi) * (x + 0.044715 * (x * x * x)))))


def _rms(x):
    return x * lax.rsqrt(jnp.mean(x * x, axis=-1, keepdims=True) + RMS_EPS)


def _tile_variant(i, n):
    return jnp.where(i == 0, 0, jnp.where(i == n - 1, 2, 1))


def _const(shape):
    zeros = (0,) * len(shape)
    return pl.BlockSpec(shape, lambda b, i: zeros, pipeline_mode=pl.Buffered(1))


def _rows(t, width):
    return pl.BlockSpec((None, t, width), lambda b, i: (b, i, 0))


def _slab(s, t):
    return pl.BlockSpec((None, None, t, D_BRANCH), lambda b, i: (s, b, i, 0))


def _adaln_kernel(c_ref, w_ref, b_ref, o_ref):
    s = _silu(c_ref[...])
    o_ref[...] = jnp.dot(s, w_ref[...], precision=lax.Precision.HIGHEST,
                         preferred_element_type=F32) + b_ref[...]


def _adaln(c_all, w_ada, b_ada):
    return pl.pallas_call(
        _adaln_kernel,
        grid=(DEPTH, 3),
        in_specs=[pl.BlockSpec((C_PAD, D_MODEL), lambda l, n: (0, 0)),
                  pl.BlockSpec((None, D_MODEL, D_MODEL), lambda l, n: (l, 0, n)),
                  pl.BlockSpec((None, 1, D_MODEL), lambda l, n: (l, 0, n))],
        out_specs=pl.BlockSpec((None, C_PAD, D_MODEL), lambda l, n: (l, 0, n)),
        out_shape=jax.ShapeDtypeStruct((DEPTH, C_PAD, 3 * D_MODEL), F32),
        compiler_params=_params(2), name="adaln",
    )(c_all, w_ada, b_ada.reshape(DEPTH, 1, 3 * D_MODEL))


@functools.lru_cache(maxsize=None)
def _pool_tables(tp):
    r = np.arange(CHUNK)[:, None]
    c = np.arange(CHUNK + 2 * POOL_HALO)[None, :]
    band = np.stack([(c >= r + POOL_HALO - w // 2) & (c < r + POOL_HALO + w // 2)
                     for w in POOL_WINDOWS]).astype(np.float32)
    t = np.arange(tp)[:, None]
    w = np.repeat(np.array(POOL_WINDOWS), D_GROUP)[None, :]
    head = np.minimum(t + w // 2, tp + w) - np.maximum(t - w // 2, 0)
    tail = np.minimum(t + w // 2, tp) - np.maximum(t - w // 2, -w)
    inv = np.stack([1.0 / head, 1.0 / np.broadcast_to(w, head.shape), 1.0 / tail])
    return band, inv.astype(np.float32)


def _gated(y, gate):
    return (y.astype(F32) * _silu(gate.astype(F32))).astype(BF16)


def _mix_out(ys, w_ref):
    y = jnp.dot(ys[0], w_ref[0:D_BRANCH, :], preferred_element_type=F32)
    for n in range(1, len(ys)):
        y = y + jnp.dot(ys[n], w_ref[n * D_BRANCH:(n + 1) * D_BRANCH, :],
                        preferred_element_type=F32)
    return y


def _modulate(x, g_ref, shift_ref, scl_ref):
    return ((_rms(x) * g_ref[...]) * (1.0 + scl_ref[...]) + shift_ref[...]).astype(BF16)


def _sgu_body(u, v, gate, g_ref, w_ref, b_ref, o_ref):
    t = u.shape[0]
    u = _gelu(u)
    v = _gelu(v)
    mu = jnp.mean(v, axis=-1, keepdims=True)
    d = v - mu
    var = jnp.mean(d * d, axis=-1, keepdims=True)
    vn = (d * lax.rsqrt(var + LN_EPS) * g_ref[...]).astype(BF16)
    gate = _silu(gate)
    head = lax.broadcasted_iota(jnp.int32, (CHUNK, D_BRANCH), 1) // D_GROUP
    zero = jnp.zeros((CHUNK, D_BRANCH), BF16)
    for n in range(t // CHUNK):
        rows = slice(n * CHUNK, (n + 1) * CHUNK)
        vc = vn[rows]
        s = None
        for p in range(N_GROUPS // 2):
            rhs = jnp.concatenate([jnp.where(head == 2 * p, vc, zero),
                                   jnp.where(head == 2 * p + 1, vc, zero)], axis=0)
            sp = jnp.dot(w_ref[p], rhs, preferred_element_type=F32)
            s = sp if s is None else s + sp
        o_ref[rows, :] = ((u[rows] * (s + b_ref[...])) * gate[rows]).astype(BF16)


def _pool_body(ext_ref, gate, band_ref, inv_ref, w_ref, scale_ref, o_ref):
    t = gate.shape[0]
    H = POOL_HALO
    gate = _silu(gate)
    group = lax.broadcasted_iota(jnp.int32, (CHUNK, D_BRANCH), 1) // D_GROUP
    for m in range(t // CHUNK):
        rows = slice(m * CHUNK, (m + 1) * CHUNK)
        e = ext_ref[m * CHUNK:(m + 1) * CHUNK + 2 * H, :]
        s = jnp.dot(band_ref[0], e, preferred_element_type=F32)
        for g in range(1, N_GROUPS):
            s = jnp.where(group == g, jnp.dot(band_ref[g], e, preferred_element_type=F32), s)
        p = s * inv_ref[rows, :] - e[H:H + CHUNK].astype(F32)
        y = jnp.dot(p.astype(BF16), w_ref[...], preferred_element_type=F32) * scale_ref[...]
        o_ref[rows, :] = (y * gate[rows]).astype(BF16)


def _layer_kernel(*refs, tm, first):
    n_stream = 2 if first else 14
    stream, rest = refs[:n_stream], refs[n_stream:]
    (g_ref, shift_ref, scl_ref, win_ref, sg_ref, sw_ref, sb_ref,
     band_ref, inv_ref, pw_ref, ps_ref) = rest[:11]
    outs = rest[11:]
    i = pl.program_id(1)
    n = pl.num_programs(1)
    H = POOL_HALO
    ext_ref, carry_ref = outs[-2:]
    slot_prev = (i + 1) % 2
    slot_this = i % 2

    @pl.when(i == 0)
    def _():
        carry_ref[slot_prev] = jnp.zeros((H, D_BRANCH), BF16)

    if first:
        x_ref, xh_ref = stream
        z_ref, ya_ref, yb_ref = outs[:3]
        h_ext = _modulate(jnp.concatenate([x_ref[...], xh_ref[...]], axis=0),
                          g_ref, shift_ref, scl_ref)
    else:
        ys, x_ref, ysh, xh_ref = stream[0:4], stream[4], stream[5:9], stream[9]
        cg_ref, cgh_ref, gprev_ref, wout_ref = stream[10:14]
        xo_ref, z_ref, ya_ref, yb_ref = outs[:4]
        ys_ext = [jnp.concatenate([m[...], mh[...]], axis=0) for m, mh in zip(ys, ysh)]
        ys_ext[2] = _gated(ys_ext[2], jnp.concatenate([cg_ref[...], cgh_ref[...]], axis=0))
        h_parts = []
        for r0 in range(0, tm, LAYER_ROWS):
            r1 = r0 + LAYER_ROWS
            if r1 == tm:
                r1 = tm + H
                x_in = jnp.concatenate([x_ref[r0:tm, :], xh_ref[...]], axis=0)
            else:
                x_in = x_ref[r0:r1, :]
            x = x_in + gprev_ref[...] * _mix_out([m[r0:r1] for m in ys_ext], wout_ref)
            xo_ref[r0:min(r1, tm), :] = x[0:min(r1, tm) - r0]
            h_parts.append(_modulate(x, g_ref, shift_ref, scl_ref))
        h_ext = jnp.concatenate(h_parts, axis=0)

    h = h_ext[0:tm]
    zl = jnp.dot(h_ext, win_ref[:, 0:Z_SLICES[0] * D_BRANCH], preferred_element_type=F32)

    def local(s):
        return zl[0:tm, s * D_BRANCH:(s + 1) * D_BRANCH]

    _sgu_body(local(B_U), local(B_V), local(B_GATE), sg_ref, sw_ref, sb_ref, yb_ref)

    a_ext = zl[:, A_IN * D_BRANCH:(A_IN + 1) * D_BRANCH].astype(BF16)
    a = a_ext[0:tm]
    a_next = a_ext[tm:tm + H]
    ext_ref[0:H, :] = carry_ref[slot_prev]
    ext_ref[H:H + tm, :] = a
    ext_ref[H + tm:2 * H + tm, :] = jnp.where(i < n - 1, a_next, jnp.zeros_like(a_next))
    carry_ref[slot_this] = a[tm - H:tm]
    _pool_body(ext_ref, local(A_GATE), band_ref, inv_ref, pw_ref, ps_ref, ya_ref)

    zg = jnp.dot(h, win_ref[:, Z_SLICES[0] * D_BRANCH:], preferred_element_type=F32)
    for k in range(len(Z_SLICES)):
        z_ref[k] = zg[:, k * D_BRANCH:(k + 1) * D_BRANCH].astype(BF16)


def _of_layer(l, shape):
    zeros = (0,) * len(shape)
    return pl.BlockSpec((None, *shape), lambda b, i: (l, *zeros), pipeline_mode=pl.Buffered(1))


def _mod_vec(l, n, b0):
    return pl.BlockSpec((None, None, None, 1, D_MODEL), lambda b, i: (l, n, b0 + b, 0, 0))


def _layer(l, b0, ys, z_prev, x, mod, prm, tm=TM):
    B, L, _ = x.shape
    first = ys is None
    H = POOL_HALO
    nt = L // tm
    nh = L // H
    band, inv = _pool_tables(tm)

    def halo(width):
        return pl.BlockSpec((None, H, width),
                            lambda b, i: (b, jnp.minimum((i + 1) * (tm // H), nh - 1), 0))

    if first:
        stream_specs = [_rows(tm, D_MODEL), halo(D_MODEL)]
        stream_args = [x, x]
    else:
        gate_halo = pl.BlockSpec(
            (None, None, H, D_BRANCH),
            lambda b, i: (ZC_GATE, b, jnp.minimum((i + 1) * (tm // H), nh - 1), 0))
        stream_specs = ([_rows(tm, D_BRANCH)] * 4 + [_rows(tm, D_MODEL)] + [halo(D_BRANCH)] * 4
                        + [halo(D_MODEL), _slab(ZC_GATE, tm), gate_halo, _mod_vec(l - 1, 2, b0),
                           _of_layer(l - 1, (D_MODEL, D_MODEL))])
        stream_args = [*ys, x, *ys, x, z_prev, z_prev, mod, prm["w_out"]]
    param_specs = [_of_layer(l, (1, D_MODEL)), _mod_vec(l, 0, b0), _mod_vec(l, 1, b0),
                   _of_layer(l, (D_MODEL, D_IN)),
                   _of_layer(l, (1, D_BRANCH)), _of_layer(l, (N_GROUPS // 2, CHUNK, 2 * CHUNK)),
                   _of_layer(l, (CHUNK, D_BRANCH)),
                   _const(band.shape),
                   pl.BlockSpec((None, tm, D_BRANCH), lambda b, i: (_tile_variant(i, nt), 0, 0)),
                   _of_layer(l, (D_BRANCH, D_BRANCH)), _of_layer(l, (1, D_BRANCH))]
    param_args = [prm["norm_g"], mod, mod, prm["w_in"],
                  prm["sgu_g"], prm["sgu_w"], prm["sgu_b"],
                  jnp.asarray(band).astype(BF16), jnp.asarray(inv), prm["pool_w"],
                  prm["pool_scale"]]
    nz = len(Z_SLICES)
    out_specs = [pl.BlockSpec((nz, None, tm, D_BRANCH), lambda b, i: (0, b, i, 0)),
                 _rows(tm, D_BRANCH), _rows(tm, D_BRANCH)]
    out_shape = [jax.ShapeDtypeStruct((nz, B, L, D_BRANCH), BF16),
                 jax.ShapeDtypeStruct((B, L, D_BRANCH), BF16),
                 jax.ShapeDtypeStruct((B, L, D_BRANCH), BF16)]
    if not first:
        out_specs = [_rows(tm, D_MODEL)] + out_specs
        out_shape = [jax.ShapeDtypeStruct((B, L, D_MODEL), F32)] + out_shape
    res = pl.pallas_call(
        functools.partial(_layer_kernel, tm=tm, first=first),
        grid=(B, nt),
        in_specs=stream_specs + param_specs,
        out_specs=out_specs,
        out_shape=out_shape,
        scratch_shapes=[pltpu.VMEM((tm + 2 * H, D_BRANCH), BF16),
                        pltpu.VMEM((2, H, D_BRANCH), BF16)],
        compiler_params=_params(2), name="layer_first" if first else "layer",
    )(*stream_args, *param_args)
    return (x, *res) if first else tuple(res)


def _dft_split(L):
    n2 = 128
    return L // n2, n2


def _dft_group(n1):
    return DFT_K // (2 * n1)


@functools.lru_cache(maxsize=None)
def _fnet_tables(L):
    n1, n2 = _dft_split(L)
    k1 = np.arange(n1, dtype=np.int64)
    n = np.arange(n2, dtype=np.int64)[:, None, None] + n2 * np.arange(n1, dtype=np.int64)[None, None, :]
    ang = 2.0 * np.pi * ((k1[None, :, None] * n) % L).astype(np.float64) / L
    gc, gs = np.cos(ang), np.sin(ang)
    gg = np.concatenate([np.concatenate([gc, -gs], axis=2),
                         np.concatenate([-gs, -gc], axis=2)], axis=1)
    jb = _dft_group(n1)
    grouped = np.zeros((n2 // jb, jb, 2 * n1, jb, 2 * n1))
    for j in range(jb):
        grouped[:, j, :, j, :] = gg[j::jb]
    gg = grouped.reshape(n2 // jb, DFT_K, DFT_K)
    kk = np.arange(n2, dtype=np.int64)
    ang2 = 2.0 * np.pi * ((kk[:, None] * kk[None, :]) % n2).astype(np.float64) / n2
    cs2 = np.concatenate([np.cos(ang2), np.sin(ang2)], axis=1)
    return gg.astype(np.float32), cs2.astype(np.float32)


@functools.lru_cache(maxsize=None)
def _channel_tables():
    c = np.arange(D_GROUP, dtype=np.int64)
    ang = 2.0 * np.pi * ((c[:, None] * c[None, :]) % D_GROUP).astype(np.float64) / D_GROUP
    eye = np.eye(N_GROUPS)
    return (np.kron(eye, np.cos(ang)).astype(np.float32),
            np.kron(eye, np.sin(ang)).astype(np.float32))


def _block_diag(w):
    eye = jnp.eye(N_GROUPS, dtype=w.dtype)
    out = jnp.einsum('...gcd,gh->...gchd', w, eye)
    return out.reshape(*w.shape[:-3], D_BRANCH, D_BRANCH)


def _fold_kernel(bdc_ref, bds_ref, w_ref, mc_ref, ms_ref):
    w = w_ref[...]
    hi = lax.Precision.HIGHEST
    mc_ref[...] = jnp.dot(bdc_ref[...], w, precision=hi, preferred_element_type=F32).astype(BF16)
    ms_ref[...] = jnp.dot(bds_ref[...], w, precision=hi, preferred_element_type=F32).astype(BF16)


def _fnet_fold(fnet_w_bd):
    bdc, bds = _channel_tables()
    full = pl.BlockSpec((D_BRANCH, D_BRANCH), lambda l: (0, 0))
    per = pl.BlockSpec((None, D_BRANCH, D_BRANCH), lambda l: (l, 0, 0))
    shp = jax.ShapeDtypeStruct((DEPTH, D_BRANCH, D_BRANCH), BF16)
    return pl.pallas_call(
        _fold_kernel, grid=(DEPTH,), in_specs=[full, full, per], out_specs=[per, per],
        out_shape=[shp, shp], compiler_params=_params(1), name="fnet_fold",
    )(jnp.asarray(bdc), jnp.asarray(bds), fnet_w_bd)


def _fnet1_kernel(x_ref, gg_ref, mc_ref, ms_ref, a_ref, *, n1, t):
    jb = _dft_group(n1)
    xt = jnp.swapaxes(x_ref[...].astype(F32), 0, 1)
    x2 = xt.reshape(t * n1, D_BRANCH).astype(BF16)
    u = jnp.dot(x2, mc_ref[...], preferred_element_type=F32)
    v = jnp.dot(x2, ms_ref[...], preferred_element_type=F32)
    groups = []
    for g in range(t // jb):
        parts = []
        for j in range(g * jb, (g + 1) * jb):
            parts += [u[j * n1:(j + 1) * n1], v[j * n1:(j + 1) * n1]]
        uv = jnp.concatenate(parts, axis=0).astype(BF16)
        a = jnp.dot(gg_ref[g], uv, preferred_element_type=F32)
        groups.append(a.reshape(jb, 2 * n1, D_BRANCH))
    a_ref[...] = jnp.swapaxes(jnp.concatenate(groups, axis=0), 0, 1).astype(BF16)


def _fnet2_kernel(ar_ref, ai_ref, cs_ref, o_ref, *, norm):
    ys = []
    for j in range(ar_ref.shape[0]):
        a = jnp.concatenate([ar_ref[j], ai_ref[j]], axis=0)
        ys.append(jnp.dot(cs_ref[...], a, preferred_element_type=F32))
    y = jnp.swapaxes(jnp.stack(ys, axis=0), 0, 1)
    o_ref[...] = (y * norm).astype(BF16)


def _fnet(l, z, mc, ms):
    nz, B, L, _ = z.shape
    n1, n2 = _dft_split(L)
    T = DFT_T
    assert L == n1 * n2 and n1 % T == 0 and DFT_K % (2 * n1) == 0, L
    gg, cs2 = _fnet_tables(L)
    gg, cs2 = jnp.asarray(gg).astype(BF16), jnp.asarray(cs2).astype(BF16)
    full = _of_layer(l, (D_BRANCH, D_BRANCH))
    t1 = min(n2, max(T, DFT_TOKENS // n1))
    a = pl.pallas_call(
        functools.partial(_fnet1_kernel, n1=n1, t=t1),
        grid=(B, n2 // t1),
        in_specs=[pl.BlockSpec((None, None, n1, t1, D_BRANCH), lambda b, i: (ZC_IN, b, 0, i, 0)),
                  pl.BlockSpec((t1 // _dft_group(n1), DFT_K, DFT_K), lambda b, i: (i, 0, 0)),
                  full, full],
        out_specs=pl.BlockSpec((None, 2 * n1, t1, D_BRANCH), lambda b, i: (b, 0, i, 0)),
        out_shape=jax.ShapeDtypeStruct((B, 2 * n1, n2, D_BRANCH), BF16),
        compiler_params=_params(2), name="fnet_stage1",
    )(z.reshape(nz, B, n1, n2, D_BRANCH), gg, mc, ms)

    t2 = min(n1, DFT_TOKENS * 2 // n2)
    nk = n1 // t2
    y = pl.pallas_call(
        functools.partial(_fnet2_kernel, norm=float(1.0 / np.sqrt(L * D_GROUP))),
        grid=(B, nk),
        in_specs=[pl.BlockSpec((None, t2, n2, D_BRANCH), lambda b, i: (b, i, 0, 0)),
                  pl.BlockSpec((None, t2, n2, D_BRANCH), lambda b, i: (b, nk + i, 0, 0)),
                  pl.BlockSpec((n2, 2 * n2), lambda b, i: (0, 0))],
        out_specs=pl.BlockSpec((None, n2, t2, D_BRANCH), lambda b, i: (b, 0, i, 0)),
        out_shape=jax.ShapeDtypeStruct((B, n2, n1, D_BRANCH), BF16),
        compiler_params=_params(2), name="fnet_stage2",
    )(a, a, cs2)
    return y.reshape(B, L, D_BRANCH)


def _na_key_col0(cb):
    return int(np.clip(cb * NA_CQ - NA_KW // 2, 0, GRID_W - NA_CK))


@functools.lru_cache(maxsize=None)
def _na_tables():
    i = np.arange(NA_TR)[:, None]
    j = np.arange(NA_KROWS)[None, :]
    row_sel = np.zeros((3, NA_TR, NA_KROWS, 2 * NA_KH - 1), np.float32)
    row_ok = np.zeros((3, NA_TR, NA_KROWS), bool)
    for var, (base, rel) in enumerate(((0, np.zeros(NA_TR, int)), (-NA_TR, np.arange(NA_TR)),
                                       (-NA_KH, np.full(NA_TR, NA_KROWS - NA_KH)))):
        ok = (j >= rel[:, None]) & (j < rel[:, None] + NA_KH)
        dr = base + j - i + NA_KH - 1
        for ii, jj in zip(*np.nonzero(ok)):
            row_sel[var, ii, jj, dr[ii, jj]] = 1.0
        row_ok[var] = ok
    col_sel = np.zeros((NA_NCB, NA_CQ, NA_CK, 2 * NA_KW - 1), np.float32)
    col_ok = np.zeros((NA_NCB, NA_CQ, NA_CK), bool)
    for cb in range(NA_NCB):
        for cq in range(NA_CQ):
            c = cb * NA_CQ + cq
            cst = int(np.clip(c - NA_KW // 2, 0, GRID_W - NA_KW))
            for kl in range(NA_CK):
                kc = _na_key_col0(cb) + kl
                if cst <= kc < cst + NA_KW:
                    col_sel[cb, cq, kl, kc - c + NA_KW - 1] = 1.0
                    col_ok[cb, cq, kl] = True
    ok = row_ok[:, None, :, None, :, None] & col_ok[None, :, None, :, None, :]
    mask = np.where(ok, 0.0, NEG_BIG).astype(np.float32).reshape(3, NA_NCB, 1, NA_BQ, NA_BK)
    nb = 2 * NA_KW - 1
    col_sel_g = np.zeros((NA_JG, nb, NA_NCB, NA_CQ, NA_JG, NA_CK), np.float32)
    for j in range(NA_JG):
        col_sel_g[j, :, :, :, j, :] = np.transpose(col_sel, (3, 0, 1, 2))
    return row_sel, col_sel_g.reshape(NA_JG * nb, NA_NCB * NA_CQ * NA_JG * NA_CK), mask


def _na_bias(na_rpb):
    row_sel, col_sel_g, mask = _na_tables()
    hi = lax.Precision.HIGHEST
    ng = NA_KROWS // NA_JG
    b = jnp.einsum('vija,lhab->lvhijb', row_sel, na_rpb, precision=hi)
    b = b.reshape(DEPTH * 3 * N_GROUPS * NA_TR, ng, col_sel_g.shape[0])
    b = jnp.einsum('rgx,xn->rgn', b, col_sel_g, precision=hi)
    b = b.reshape(DEPTH, 3, N_GROUPS, NA_TR, ng, NA_NCB, NA_CQ, NA_JG * NA_CK)
    b = jnp.transpose(b, (0, 1, 5, 2, 3, 6, 4, 7))
    b = b.reshape(DEPTH, 3, NA_NCB, N_GROUPS, NA_BQ, NA_BK) + mask
    return b.reshape(DEPTH, 3, NA_NCB, N_GROUPS * NA_BQ, NA_BK)


def _na_first_block(i, nt, nwin):
    return jnp.clip(i * NA_SUB - 1, 0, nt - nwin)


def _na_kernel(*refs, nt, nwin):
    q_ref, gate_ref = refs[:2]
    k_refs, v_refs = refs[5:5 + nwin], refs[5 + nwin:5 + 2 * nwin]
    o_ref, acc_ref, k_all, v_all = refs[-4:]
    i = pl.program_id(1)
    nkb = NA_KROWS // NA_TR
    for d in range(nwin):
        k_all[d * NA_TQ:(d + 1) * NA_TQ, :] = k_refs[d][...].astype(F32)
        v_all[d * NA_TQ:(d + 1) * NA_TQ, :] = v_refs[d][...].astype(F32)
    first_block = _na_first_block(i, nt, nwin)
    for t in range(NA_SUB):
        bias_ref = refs[2 + (0 if t == 0 else 2 if t == NA_SUB - 1 else 1)]
        row0 = (jnp.clip(i * NA_SUB + t - 1, 0, nt - nkb) - first_block) * NA_TQ
        _na_tile(q_ref, t * NA_TQ, k_all, v_all, row0, bias_ref, acc_ref)
    o_ref[...] = (acc_ref[...] * _silu(gate_ref[...].astype(F32))).astype(BF16)


def _na_tile(q_ref, q0, k_all, v_all, key0, bias_ref, acc_ref):
    head = lax.broadcasted_iota(jnp.int32, (NA_BQ, D_BRANCH), 1) // D_GROUP
    for cb in range(NA_NCB):
        c0 = q0 + cb * NA_CQ
        kc0 = _na_key_col0(cb)
        q = jnp.concatenate([q_ref[i * GRID_W + c0:i * GRID_W + c0 + NA_CQ, :]
                             for i in range(NA_TR)], axis=0)
        q = q * jnp.asarray(D_GROUP ** -0.5, BF16)
        zero = jnp.zeros_like(q)
        qs = jnp.concatenate([jnp.where(head == h, q, zero) for h in range(N_GROUPS)], axis=0)
        rows = [pl.ds(pl.multiple_of(key0 + j * GRID_W + kc0, 8), NA_CK) for j in range(NA_KROWS)]
        kb = jnp.concatenate([k_all[r, :] for r in rows], axis=0).astype(BF16)
        vb = jnp.concatenate([v_all[r, :] for r in rows], axis=0).astype(BF16)
        s = lax.dot_general(qs, kb, (((1,), (1,)), ((), ())), preferred_element_type=F32)
        s = s + bias_ref[cb]
        m = jnp.max(s, axis=-1, keepdims=True)
        p = jnp.exp(s - m)
        denom = jnp.sum(p, axis=-1, keepdims=True)
        o = jnp.dot(p.astype(BF16), vb, preferred_element_type=F32) / denom
        out = o[0:NA_BQ]
        for h in range(1, N_GROUPS):
            out = jnp.where(head == h, o[h * NA_BQ:(h + 1) * NA_BQ], out)
        for i in range(NA_TR):
            acc_ref[i * GRID_W + c0:i * GRID_W + c0 + NA_CQ, :] = out[i * NA_CQ:(i + 1) * NA_CQ]


def _na(l, z, bias):
    _, B, L, _ = z.shape
    nt = L // NA_TQ
    nkb = NA_KROWS // NA_TR
    assert nt % NA_SUB == 0 and nt >= nkb, L
    nwin = min(NA_SUB + nkb - 1, nt)

    def kv(s, d):
        return pl.BlockSpec((None, None, NA_TQ, D_BRANCH),
                            lambda b, i: (s, b, _na_first_block(i, nt, nwin) + d, 0))

    def bias_of(t):
        return pl.BlockSpec(
            (None, None, NA_NCB, N_GROUPS * NA_BQ, NA_BK),
            lambda b, i: (l, _tile_variant(i * NA_SUB + t, nt), 0, 0, 0))

    bias_inner = pl.BlockSpec((None, None, NA_NCB, N_GROUPS * NA_BQ, NA_BK),
                              lambda b, i: (l, 1, 0, 0, 0), pipeline_mode=pl.Buffered(1))
    return pl.pallas_call(
        functools.partial(_na_kernel, nt=nt, nwin=nwin),
        grid=(B, nt // NA_SUB),
        in_specs=[_slab(ZD_Q, NA_SUB * NA_TQ), _slab(ZD_GATE, NA_SUB * NA_TQ),
                  bias_of(0), bias_inner, bias_of(NA_SUB - 1)]
        + [kv(ZD_K, d) for d in range(nwin)] + [kv(ZD_V, d) for d in range(nwin)],
        out_specs=_rows(NA_SUB * NA_TQ, D_BRANCH),
        out_shape=jax.ShapeDtypeStruct((B, L, D_BRANCH), BF16),
        scratch_shapes=[pltpu.VMEM((NA_SUB * NA_TQ, D_BRANCH), F32),
                        pltpu.VMEM((nwin * NA_TQ, D_BRANCH), F32),
                        pltpu.VMEM((nwin * NA_TQ, D_BRANCH), F32)],
        compiler_params=_params(2), name="na",
    )(z, z, bias, bias, bias, *([z] * (2 * nwin)))


def _final_kernel(ya_ref, yb_ref, yc_ref, yd_ref, cg_ref, x_ref, gate_ref, w_ref, fg_ref, o_ref):
    y = _mix_out([ya_ref[...], yb_ref[...], _gated(yc_ref[...], cg_ref[...]), yd_ref[...]], w_ref)
    o_ref[...] = _rms(x_ref[...] + gate_ref[...] * y) * fg_ref[...]


def _final(b0, ys, z_prev, x, mod, prm, final_g, tm=TM):
    B, L, _ = x.shape
    last = DEPTH - 1
    return pl.pallas_call(
        _final_kernel,
        grid=(B, L // tm),
        in_specs=[_rows(tm, D_BRANCH)] * 4 + [
            _slab(ZC_GATE, tm), _rows(tm, D_MODEL), _mod_vec(last, 2, b0),
            _of_layer(last, (D_MODEL, D_MODEL)), _const((1, D_MODEL))],
        out_specs=_rows(tm, D_MODEL),
        out_shape=jax.ShapeDtypeStruct((B, L, D_MODEL), F32),
        compiler_params=_params(2), name="final",
    )(*ys, z_prev, x, mod, prm["w_out"], final_g.reshape(1, D_MODEL))


def _trunk(x, b0, mod, prm, final_norm_g):
    ys, z = None, None
    for l in range(DEPTH):
        x, z, ya, yb = _layer(l, b0, ys, z, x, mod, prm)
        yc = _fnet(l, z, prm["fnet_mc"], prm["fnet_ms"])
        yd = _na(l, z, prm["na_bias"])
        ys = (ya, yb, yc, yd)
    return _final(b0, ys, z, x, mod, prm, final_norm_g)


def kernel(x_prompt, x_sample, c_prompt, c_sample, norm_g, w_ada, b_ada, w_in, w_out, pool_w, pool_scale, sgu_norm_g, sgu_w, sgu_b, fnet_w, na_rpb, final_norm_g):
    nb_p, nb_s = c_prompt.shape[0], c_sample.shape[0]
    c_all = jnp.concatenate(
        [c_prompt, c_sample, jnp.zeros((C_PAD - nb_p - nb_s, D_MODEL), F32)], axis=0)
    mod = _adaln(c_all, w_ada, b_ada)
    mod = jnp.transpose(mod.reshape(DEPTH, C_PAD, 3, 1, D_MODEL), (0, 2, 1, 3, 4))

    mc, ms = _fnet_fold(_block_diag(fnet_w))
    prm = {
        "norm_g": norm_g.reshape(DEPTH, 1, D_MODEL),
        "sgu_g": sgu_norm_g.reshape(DEPTH, 1, D_BRANCH),
        "pool_scale": pool_scale.reshape(DEPTH, 1, D_BRANCH),
        "w_in": w_in.astype(BF16),
        "w_out": w_out.astype(BF16),
        "pool_w": _block_diag(pool_w).astype(BF16),
        "sgu_w": jnp.swapaxes(sgu_w.reshape(DEPTH, N_GROUPS // 2, 2, CHUNK, CHUNK), 2, 3)
        .reshape(DEPTH, N_GROUPS // 2, CHUNK, 2 * CHUNK).astype(BF16),
        "sgu_b": jnp.repeat(jnp.swapaxes(sgu_b, 1, 2), D_GROUP, axis=2),
        "fnet_mc": mc,
        "fnet_ms": ms,
        "na_bias": _na_bias(na_rpb),
    }
    y_prompt = _trunk(x_prompt, 0, mod, prm, final_norm_g)
    y_sample = _trunk(x_sample, nb_p, mod, prm, final_norm_g)
    return (y_prompt, y_sample)
```

```python
import functools

import numpy as np
import jax
import jax.numpy as jnp
from jax import lax
from jax.experimental import pallas as pl
from jax.experimental.pallas import tpu as pltpu

F32 = jnp.float32
BF16 = jnp.bfloat16
BF16_ROWS = 16

D_MODEL = 1024
DEPTH = 4
D_BRANCH = 256
N_GROUPS = 4
D_GROUP = 64
POOL_WINDOWS = (2, 4, 8, 16)
POOL_HALO = BF16_ROWS
CHUNK = 128
GRID_W = 64
NA_KH = 8
NA_KW = 16
N_IN_SLICES = 11
D_IN = N_IN_SLICES * D_BRANCH
RMS_EPS = 1e-6
LN_EPS = 1e-5
NEG_BIG = -1e30

A_IN, A_GATE, B_U, B_V, B_GATE, C_IN, C_GATE, D_Q, D_K, D_V, D_GATE = range(N_IN_SLICES)
Z_SLICES = (C_IN, C_GATE, D_Q, D_K, D_V, D_GATE)
ZC_IN, ZC_GATE, ZD_Q, ZD_K, ZD_V, ZD_GATE = range(len(Z_SLICES))

NA_TR = 4
NA_KROWS = NA_TR + NA_KH
NA_TQ = NA_TR * GRID_W
NA_SUB = 8
NA_CQ = 16
NA_NCB = GRID_W // NA_CQ
NA_CK = 2 * NA_KW
NA_BQ = NA_TR * NA_CQ
NA_BK = NA_KROWS * NA_CK
NA_JG = 128 // NA_CK
C_PAD = 16
DFT_T = BF16_ROWS
DFT_K = 256
DFT_TOKENS = 2048
TM = 1024
LAYER_ROWS = 256

VMEM_LIMIT = 56 * 1024 * 1024


def _params(n_axes, vmem=VMEM_LIMIT):
    return pltpu.CompilerParams(dimension_semantics=("arbitrary",) * n_axes,
                                vmem_limit_bytes=vmem)


def _silu(x):
    return x / (1.0 + jnp.exp(-x))


def _gelu(x):
    return x * (0.5 * (1.0 + jnp.tanh(np.sqrt(2.0 / np.pi) * (x + 0.044715 * (x * x * x)))))


def _rms(x):
    return x * lax.rsqrt(jnp.mean(x * x, axis=-1, keepdims=True) + RMS_EPS)


def _tile_variant(i, n):
    return jnp.where(i == 0, 0, jnp.where(i == n - 1, 2, 1))


def _const(shape):
    zeros = (0,) * len(shape)
    return pl.BlockSpec(shape, lambda b, i: zeros, pipeline_mode=pl.Buffered(1))


def _rows(t, width):
    return pl.BlockSpec((None, t, width), lambda b, i: (b, i, 0))


def _slab(s, t):
    return pl.BlockSpec((None, None, t, D_BRANCH), lambda b, i: (s, b, i, 0))


def _adaln_kernel(c_ref, w_ref, b_ref, o_ref):
    s = _silu(c_ref[...])
    o_ref[...] = jnp.dot(s, w_ref[...], precision=lax.Precision.HIGHEST,
                         preferred_element_type=F32) + b_ref[...]


def _adaln(c_all, w_ada, b_ada):
    return pl.pallas_call(
        _adaln_kernel,
        grid=(DEPTH, 3),
        in_specs=[pl.BlockSpec((C_PAD, D_MODEL), lambda l, n: (0, 0)),
                  pl.BlockSpec((None, D_MODEL, D_MODEL), lambda l, n: (l, 0, n)),
                  pl.BlockSpec((None, 1, D_MODEL), lambda l, n: (l, 0, n))],
        out_specs=pl.BlockSpec((None, C_PAD, D_MODEL), lambda l, n: (l, 0, n)),
        out_shape=jax.ShapeDtypeStruct((DEPTH, C_PAD, 3 * D_MODEL), F32),
        compiler_params=_params(2), name="adaln",
    )(c_all, w_ada, b_ada.reshape(DEPTH, 1, 3 * D_MODEL))


@functools.lru_cache(maxsize=None)
def _pool_tables(tp):
    r = np.arange(CHUNK)[:, None]
    c = np.arange(CHUNK + 2 * POOL_HALO)[None, :]
    band = np.stack([(c >= r + POOL_HALO - w // 2) & (c < r + POOL_HALO + w // 2)
                     for w in POOL_WINDOWS]).astype(np.float32)
    t = np.arange(tp)[:, None]
    w = np.repeat(np.array(POOL_WINDOWS), D_GROUP)[None, :]
    head = np.minimum(t + w // 2, tp + w) - np.maximum(t - w // 2, 0)
    tail = np.minimum(t + w // 2, tp) - np.maximum(t - w // 2, -w)
    inv = np.stack([1.0 / head, 1.0 / np.broadcast_to(w, head.shape), 1.0 / tail])
    return band, inv.astype(np.float32)


def _gated(y, gate):
    return (y.astype(F32) * _silu(gate.astype(F32))).astype(BF16)


def _mix_out(ys, w_ref):
    y = jnp.dot(ys[0], w_ref[0:D_BRANCH, :], preferred_element_type=F32)
    for n in range(1, len(ys)):
        y = y + jnp.dot(ys[n], w_ref[n * D_BRANCH:(n + 1) * D_BRANCH, :],
                        preferred_element_type=F32)
    return y


def _modulate(x, g_ref, shift_ref, scl_ref):
    return ((_rms(x) * g_ref[...]) * (1.0 + scl_ref[...]) + shift_ref[...]).astype(BF16)


def _sgu_body(u, v, gate, g_ref, w_ref, b_ref, o_ref):
    t = u.shape[0]
    u = _gelu(u)
    v = _gelu(v)
    mu = jnp.mean(v, axis=-1, keepdims=True)
    d = v - mu
    var = jnp.mean(d * d, axis=-1, keepdims=True)
    vn = (d * lax.rsqrt(var + LN_EPS) * g_ref[...]).astype(BF16)
    gate = _silu(gate)
    head = lax.broadcasted_iota(jnp.int32, (CHUNK, D_BRANCH), 1) // D_GROUP
    zero = jnp.zeros((CHUNK, D_BRANCH), BF16)
    for n in range(t // CHUNK):
        rows = slice(n * CHUNK, (n + 1) * CHUNK)
        vc = vn[rows]
        s = None
        for p in range(N_GROUPS // 2):
            rhs = jnp.concatenate([jnp.where(head == 2 * p, vc, zero),
                                   jnp.where(head == 2 * p + 1, vc, zero)], axis=0)
            sp = jnp.dot(w_ref[p], rhs, preferred_element_type=F32)
            s = sp if s is None else s + sp
        o_ref[rows, :] = ((u[rows] * (s + b_ref[...])) * gate[rows]).astype(BF16)


def _pool_body(ext_ref, gate, band_ref, inv_ref, w_ref, scale_ref, o_ref):
    t = gate.shape[0]
    H = POOL_HALO
    gate = _silu(gate)
    group = lax.broadcasted_iota(jnp.int32, (CHUNK, D_BRANCH), 1) // D_GROUP
    for m in range(t // CHUNK):
        rows = slice(m * CHUNK, (m + 1) * CHUNK)
        e = ext_ref[m * CHUNK:(m + 1) * CHUNK + 2 * H, :]
        s = jnp.dot(band_ref[0], e, preferred_element_type=F32)
        for g in range(1, N_GROUPS):
            s = jnp.where(group == g, jnp.dot(band_ref[g], e, preferred_element_type=F32), s)
        p = s * inv_ref[rows, :] - e[H:H + CHUNK].astype(F32)
        y = jnp.dot(p.astype(BF16), w_ref[...], preferred_element_type=F32) * scale_ref[...]
        o_ref[rows, :] = (y * gate[rows]).astype(BF16)


def _layer_kernel(*refs, tm, first):
    n_stream = 2 if first else 14
    stream, rest = refs[:n_stream], refs[n_stream:]
    (g_ref, shift_ref, scl_ref, win_ref, sg_ref, sw_ref, sb_ref,
     band_ref, inv_ref, pw_ref, ps_ref) = rest[:11]
    outs = rest[11:]
    i = pl.program_id(1)
    n = pl.num_programs(1)
    H = POOL_HALO
    ext_ref, carry_ref = outs[-2:]
    slot_prev = (i + 1) % 2
    slot_this = i % 2

    @pl.when(i == 0)
    def _():
        carry_ref[slot_prev] = jnp.zeros((H, D_BRANCH), BF16)

    if first:
        x_ref, xh_ref = stream
        z_ref, ya_ref, yb_ref = outs[:3]
        h_ext = _modulate(jnp.concatenate([x_ref[...], xh_ref[...]], axis=0),
                          g_ref, shift_ref, scl_ref)
    else:
        ys, x_ref, ysh, xh_ref = stream[0:4], stream[4], stream[5:9], stream[9]
        cg_ref, cgh_ref, gprev_ref, wout_ref = stream[10:14]
        xo_ref, z_ref, ya_ref, yb_ref = outs[:4]
        ys_ext = [jnp.concatenate([m[...], mh[...]], axis=0) for m, mh in zip(ys, ysh)]
        ys_ext[2] = _gated(ys_ext[2], jnp.concatenate([cg_ref[...], cgh_ref[...]], axis=0))
        h_parts = []
        for r0 in range(0, tm, LAYER_ROWS):
            r1 = r0 + LAYER_ROWS
            if r1 == tm:
                r1 = tm + H
                x_in = jnp.concatenate([x_ref[r0:tm, :], xh_ref[...]], axis=0)
            else:
                x_in = x_ref[r0:r1, :]
            x = x_in + gprev_ref[...] * _mix_out([m[r0:r1] for m in ys_ext], wout_ref)
            xo_ref[r0:min(r1, tm), :] = x[0:min(r1, tm) - r0]
            h_parts.append(_modulate(x, g_ref, shift_ref, scl_ref))
        h_ext = jnp.concatenate(h_parts, axis=0)

    h = h_ext[0:tm]
    zl = jnp.dot(h_ext, win_ref[:, 0:Z_SLICES[0] * D_BRANCH], preferred_element_type=F32)

    def local(s):
        return zl[0:tm, s * D_BRANCH:(s + 1) * D_BRANCH]

    _sgu_body(local(B_U), local(B_V), local(B_GATE), sg_ref, sw_ref, sb_ref, yb_ref)

    a_ext = zl[:, A_IN * D_BRANCH:(A_IN + 1) * D_BRANCH].astype(BF16)
    a = a_ext[0:tm]
    a_next = a_ext[tm:tm + H]
    ext_ref[0:H, :] = carry_ref[slot_prev]
    ext_ref[H:H + tm, :] = a
    ext_ref[H + tm:2 * H + tm, :] = jnp.where(i < n - 1, a_next, jnp.zeros_like(a_next))
    carry_ref[slot_this] = a[tm - H:tm]
    _pool_body(ext_ref, local(A_GATE), band_ref, inv_ref, pw_ref, ps_ref, ya_ref)

    zg = jnp.dot(h, win_ref[:, Z_SLICES[0] * D_BRANCH:], preferred_element_type=F32)
    for k in range(len(Z_SLICES)):
        z_ref[k] = zg[:, k * D_BRANCH:(k + 1) * D_BRANCH].astype(BF16)


def _of_layer(l, shape):
    zeros = (0,) * len(shape)
    return pl.BlockSpec((None, *shape), lambda b, i: (l, *zeros), pipeline_mode=pl.Buffered(1))


def _mod_vec(l, n, b0):
    return pl.BlockSpec((None, None, None, 1, D_MODEL), lambda b, i: (l, n, b0 + b, 0, 0))


def _layer(l, b0, ys, z_prev, x, mod, prm, tm=TM):
    B, L, _ = x.shape
    first = ys is None
    H = POOL_HALO
    nt = L // tm
    nh = L // H
    band, inv = _pool_tables(tm)

    def halo(width):
        return pl.BlockSpec((None, H, width),
                            lambda b, i: (b, jnp.minimum((i + 1) * (tm // H), nh - 1), 0))

    if first:
        stream_specs = [_rows(tm, D_MODEL), halo(D_MODEL)]
        stream_args = [x, x]
    else:
        gate_halo = pl.BlockSpec(
            (None, None, H, D_BRANCH),
            lambda b, i: (ZC_GATE, b, jnp.minimum((i + 1) * (tm // H), nh - 1), 0))
        stream_specs = ([_rows(tm, D_BRANCH)] * 4 + [_rows(tm, D_MODEL)] + [halo(D_BRANCH)] * 4
                        + [halo(D_MODEL), _slab(ZC_GATE, tm), gate_halo, _mod_vec(l - 1, 2, b0),
                           _of_layer(l - 1, (D_MODEL, D_MODEL))])
        stream_args = [*ys, x, *ys, x, z_prev, z_prev, mod, prm["w_out"]]
    param_specs = [_of_layer(l, (1, D_MODEL)), _mod_vec(l, 0, b0), _mod_vec(l, 1, b0),
                   _of_layer(l, (D_MODEL, D_IN)),
                   _of_layer(l, (1, D_BRANCH)), _of_layer(l, (N_GROUPS // 2, CHUNK, 2 * CHUNK)),
                   _of_layer(l, (CHUNK, D_BRANCH)),
                   _const(band.shape),
                   pl.BlockSpec((None, tm, D_BRANCH), lambda b, i: (_tile_variant(i, nt), 0, 0)),
                   _of_layer(l, (D_BRANCH, D_BRANCH)), _of_layer(l, (1, D_BRANCH))]
    param_args = [prm["norm_g"], mod, mod, prm["w_in"],
                  prm["sgu_g"], prm["sgu_w"], prm["sgu_b"],
                  jnp.asarray(band).astype(BF16), jnp.asarray(inv), prm["pool_w"],
                  prm["pool_scale"]]
    nz = len(Z_SLICES)
    out_specs = [pl.BlockSpec((nz, None, tm, D_BRANCH), lambda b, i: (0, b, i, 0)),
                 _rows(tm, D_BRANCH), _rows(tm, D_BRANCH)]
    out_shape = [jax.ShapeDtypeStruct((nz, B, L, D_BRANCH), BF16),
                 jax.ShapeDtypeStruct((B, L, D_BRANCH), BF16),
                 jax.ShapeDtypeStruct((B, L, D_BRANCH), BF16)]
    if not first:
        out_specs = [_rows(tm, D_MODEL)] + out_specs
        out_shape = [jax.ShapeDtypeStruct((B, L, D_MODEL), F32)] + out_shape
    res = pl.pallas_call(
        functools.partial(_layer_kernel, tm=tm, first=first),
        grid=(B, nt),
        in_specs=stream_specs + param_specs,
        out_specs=out_specs,
        out_shape=out_shape,
        scratch_shapes=[pltpu.VMEM((tm + 2 * H, D_BRANCH), BF16),
                        pltpu.VMEM((2, H, D_BRANCH), BF16)],
        compiler_params=_params(2), name="layer_first" if first else "layer",
    )(*stream_args, *param_args)
    return (x, *res) if first else tuple(res)


def _dft_split(L):
    n2 = 128
    return L // n2, n2


def _dft_group(n1):
    return DFT_K // (2 * n1)


@functools.lru_cache(maxsize=None)
def _fnet_tables(L):
    n1, n2 = _dft_split(L)
    k1 = np.arange(n1, dtype=np.int64)
    n = np.arange(n2, dtype=np.int64)[:, None, None] + n2 * np.arange(n1, dtype=np.int64)[None, None, :]
    ang = 2.0 * np.pi * ((k1[None, :, None] * n) % L).astype(np.float64) / L
    gc, gs = np.cos(ang), np.sin(ang)
    gg = np.concatenate([np.concatenate([gc, -gs], axis=2),
                         np.concatenate([-gs, -gc], axis=2)], axis=1)
    jb = _dft_group(n1)
    grouped = np.zeros((n2 // jb, jb, 2 * n1, jb, 2 * n1))
    for j in range(jb):
        grouped[:, j, :, j, :] = gg[j::jb]
    gg = grouped.reshape(n2 // jb, DFT_K, DFT_K)
    kk = np.arange(n2, dtype=np.int64)
    ang2 = 2.0 * np.pi * ((kk[:, None] * kk[None, :]) % n2).astype(np.float64) / n2
    cs2 = np.concatenate([np.cos(ang2), np.sin(ang2)], axis=1)
    return gg.astype(np.float32), cs2.astype(np.float32)


@functools.lru_cache(maxsize=None)
def _channel_tables():
    c = np.arange(D_GROUP, dtype=np.int64)
    ang = 2.0 * np.pi * ((c[:, None] * c[None, :]) % D_GROUP).astype(np.float64) / D_GROUP
    eye = np.eye(N_GROUPS)
    return (np.kron(eye, np.cos(ang)).astype(np.float32),
            np.kron(eye, np.sin(ang)).astype(np.float32))


def _block_diag(w):
    eye = jnp.eye(N_GROUPS, dtype=w.dtype)
    out = jnp.einsum('...gcd,gh->...gchd', w, eye)
    return out.reshape(*w.shape[:-3], D_BRANCH, D_BRANCH)


def _fold_kernel(bdc_ref, bds_ref, w_ref, mc_ref, ms_ref):
    w = w_ref[...]
    hi = lax.Precision.HIGHEST
    mc_ref[...] = jnp.dot(bdc_ref[...], w, precision=hi, preferred_element_type=F32).astype(BF16)
    ms_ref[...] = jnp.dot(bds_ref[...], w, precision=hi, preferred_element_type=F32).astype(BF16)


def _fnet_fold(fnet_w_bd):
    bdc, bds = _channel_tables()
    full = pl.BlockSpec((D_BRANCH, D_BRANCH), lambda l: (0, 0))
    per = pl.BlockSpec((None, D_BRANCH, D_BRANCH), lambda l: (l, 0, 0))
    shp = jax.ShapeDtypeStruct((DEPTH, D_BRANCH, D_BRANCH), BF16)
    return pl.pallas_call(
        _fold_kernel, grid=(DEPTH,), in_specs=[full, full, per], out_specs=[per, per],
        out_shape=[shp, shp], compiler_params=_params(1), name="fnet_fold",
    )(jnp.asarray(bdc), jnp.asarray(bds), fnet_w_bd)


def _fnet1_kernel(x_ref, gg_ref, mc_ref, ms_ref, a_ref, *, n1, t):
    jb = _dft_group(n1)
    xt = jnp.swapaxes(x_ref[...].astype(F32), 0, 1)
    x2 = xt.reshape(t * n1, D_BRANCH).astype(BF16)
    u = jnp.dot(x2, mc_ref[...], preferred_element_type=F32)
    v = jnp.dot(x2, ms_ref[...], preferred_element_type=F32)
    groups = []
    for g in range(t // jb):
        parts = []
        for j in range(g * jb, (g + 1) * jb):
            parts += [u[j * n1:(j + 1) * n1], v[j * n1:(j + 1) * n1]]
        uv = jnp.concatenate(parts, axis=0).astype(BF16)
        a = jnp.dot(gg_ref[g], uv, preferred_element_type=F32)
        groups.append(a.reshape(jb, 2 * n1, D_BRANCH))
    a_ref[...] = jnp.swapaxes(jnp.concatenate(groups, axis=0), 0, 1).astype(BF16)


def _fnet2_kernel(ar_ref, ai_ref, cs_ref, o_ref, *, norm):
    ys = []
    for j in range(ar_ref.shape[0]):
        a = jnp.concatenate([ar_ref[j], ai_ref[j]], axis=0)
        ys.append(jnp.dot(cs_ref[...], a, preferred_element_type=F32))
    y = jnp.swapaxes(jnp.stack(ys, axis=0), 0, 1)
    o_ref[...] = (y * norm).astype(BF16)


def _fnet(l, z, mc, ms):
    nz, B, L, _ = z.shape
    n1, n2 = _dft_split(L)
    T = DFT_T
    assert L == n1 * n2 and n1 % T == 0 and DFT_K % (2 * n1) == 0, L
    gg, cs2 = _fnet_tables(L)
    gg, cs2 = jnp.asarray(gg).astype(BF16), jnp.asarray(cs2).astype(BF16)
    full = _of_layer(l, (D_BRANCH, D_BRANCH))
    t1 = min(n2, max(T, DFT_TOKENS // n1))
    a = pl.pallas_call(
        functools.partial(_fnet1_kernel, n1=n1, t=t1),
        grid=(B, n2 // t1),
        in_specs=[pl.BlockSpec((None, None, n1, t1, D_BRANCH), lambda b, i: (ZC_IN, b, 0, i, 0)),
                  pl.BlockSpec((t1 // _dft_group(n1), DFT_K, DFT_K), lambda b, i: (i, 0, 0)),
                  full, full],
        out_specs=pl.BlockSpec((None, 2 * n1, t1, D_BRANCH), lambda b, i: (b, 0, i, 0)),
        out_shape=jax.ShapeDtypeStruct((B, 2 * n1, n2, D_BRANCH), BF16),
        compiler_params=_params(2), name="fnet_stage1",
    )(z.reshape(nz, B, n1, n2, D_BRANCH), gg, mc, ms)

    t2 = min(n1, DFT_TOKENS * 2 // n2)
    nk = n1 // t2
    y = pl.pallas_call(
        functools.partial(_fnet2_kernel, norm=float(1.0 / np.sqrt(L * D_GROUP))),
        grid=(B, nk),
        in_specs=[pl.BlockSpec((None, t2, n2, D_BRANCH), lambda b, i: (b, i, 0, 0)),
                  pl.BlockSpec((None, t2, n2, D_BRANCH), lambda b, i: (b, nk + i, 0, 0)),
                  pl.BlockSpec((n2, 2 * n2), lambda b, i: (0, 0))],
        out_specs=pl.BlockSpec((None, n2, t2, D_BRANCH), lambda b, i: (b, 0, i, 0)),
        out_shape=jax.ShapeDtypeStruct((B, n2, n1, D_BRANCH), BF16),
        compiler_params=_params(2), name="fnet_stage2",
    )(a, a, cs2)
    return y.reshape(B, L, D_BRANCH)


def _na_key_col0(cb):
    return int(np.clip(cb * NA_CQ - NA_KW // 2, 0, GRID_W - NA_CK))


@functools.lru_cache(maxsize=None)
def _na_tables():
    i = np.arange(NA_TR)[:, None]
    j = np.arange(NA_KROWS)[None, :]
    row_sel = np.zeros((3, NA_TR, NA_KROWS, 2 * NA_KH - 1), np.float32)
    row_ok = np.zeros((3, NA_TR, NA_KROWS), bool)
    for var, (base, rel) in enumerate(((0, np.zeros(NA_TR, int)), (-NA_TR, np.arange(NA_TR)),
                                       (-NA_KH, np.full(NA_TR, NA_KROWS - NA_KH)))):
        ok = (j >= rel[:, None]) & (j < rel[:, None] + NA_KH)
        dr = base + j - i + NA_KH - 1
        for ii, jj in zip(*np.nonzero(ok)):
            row_sel[var, ii, jj, dr[ii, jj]] = 1.0
        row_ok[var] = ok
    col_sel = np.zeros((NA_NCB, NA_CQ, NA_CK, 2 * NA_KW - 1), np.float32)
    col_ok = np.zeros((NA_NCB, NA_CQ, NA_CK), bool)
    for cb in range(NA_NCB):
        for cq in range(NA_CQ):
            c = cb * NA_CQ + cq
            cst = int(np.clip(c - NA_KW // 2, 0, GRID_W - NA_KW))
            for kl in range(NA_CK):
                kc = _na_key_col0(cb) + kl
                if cst <= kc < cst + NA_KW:
                    col_sel[cb, cq, kl, kc - c + NA_KW - 1] = 1.0
                    col_ok[cb, cq, kl] = True
    ok = row_ok[:, None, :, None, :, None] & col_ok[None, :, None, :, None, :]
    mask = np.where(ok, 0.0, NEG_BIG).astype(np.float32).reshape(3, NA_NCB, 1, NA_BQ, NA_BK)
    nb = 2 * NA_KW - 1
    col_sel_g = np.zeros((NA_JG, nb, NA_NCB, NA_CQ, NA_JG, NA_CK), np.float32)
    for j in range(NA_JG):
        col_sel_g[j, :, :, :, j, :] = np.transpose(col_sel, (3, 0, 1, 2))
    return row_sel, col_sel_g.reshape(NA_JG * nb, NA_NCB * NA_CQ * NA_JG * NA_CK), mask


def _na_bias(na_rpb):
    row_sel, col_sel_g, mask = _na_tables()
    hi = lax.Precision.HIGHEST
    ng = NA_KROWS // NA_JG
    b = jnp.einsum('vija,lhab->lvhijb', row_sel, na_rpb, precision=hi)
    b = b.reshape(DEPTH * 3 * N_GROUPS * NA_TR, ng, col_sel_g.shape[0])
    b = jnp.einsum('rgx,xn->rgn', b, col_sel_g, precision=hi)
    b = b.reshape(DEPTH, 3, N_GROUPS, NA_TR, ng, NA_NCB, NA_CQ, NA_JG * NA_CK)
    b = jnp.transpose(b, (0, 1, 5, 2, 3, 6, 4, 7))
    b = b.reshape(DEPTH, 3, NA_NCB, N_GROUPS, NA_BQ, NA_BK) + mask
    return b.reshape(DEPTH, 3, NA_NCB, N_GROUPS * NA_BQ, NA_BK)


def _na_first_block(i, nt, nwin):
    return jnp.clip(i * NA_SUB - 1, 0, nt - nwin)


def _na_kernel(*refs, nt, nwin):
    q_ref, gate_ref = refs[:2]
    k_refs, v_refs = refs[5:5 + nwin], refs[5 + nwin:5 + 2 * nwin]
    o_ref, k_all, v_all = refs[-3:]
    i = pl.program_id(1)
    nkb = NA_KROWS // NA_TR
    n_rows = nwin * NA_TQ
    for d in range(nwin):
        k_all[0, d * NA_TQ:(d + 1) * NA_TQ, :] = k_refs[d][...]
        v_all[0, d * NA_TQ:(d + 1) * NA_TQ, :] = v_refs[d][...]
    half = BF16_ROWS // 2
    k_all[1, 0:n_rows - BF16_ROWS, :] = k_all[0][half:n_rows - half, :]
    v_all[1, 0:n_rows - BF16_ROWS, :] = v_all[0][half:n_rows - half, :]
    first_block = _na_first_block(i, nt, nwin)
    for t in range(NA_SUB):
        bias_ref = refs[2 + (0 if t == 0 else 2 if t == NA_SUB - 1 else 1)]
        row0 = (jnp.clip(i * NA_SUB + t - 1, 0, nt - nkb) - first_block) * NA_TQ
        _na_tile(q_ref, gate_ref, t * NA_TQ, k_all, v_all, row0, bias_ref, o_ref)


def _na_tile(q_ref, gate_ref, q0, k_all, v_all, key0, bias_ref, o_ref):
    head = lax.broadcasted_iota(jnp.int32, (NA_BQ, D_BRANCH), 1) // D_GROUP
    for cb in range(NA_NCB):
        c0 = q0 + cb * NA_CQ
        kc0 = _na_key_col0(cb)
        q = jnp.concatenate([q_ref[i * GRID_W + c0:i * GRID_W + c0 + NA_CQ, :]
                             for i in range(NA_TR)], axis=0)
        q = q * jnp.asarray(D_GROUP ** -0.5, BF16)
        zero = jnp.zeros_like(q)
        qs = jnp.concatenate([jnp.where(head == h, q, zero) for h in range(N_GROUPS)], axis=0)
        slot = (kc0 % BF16_ROWS) // (BF16_ROWS // 2)
        col = kc0 - slot * (BF16_ROWS // 2)
        rows = [pl.ds(pl.multiple_of(key0 + j * GRID_W + col, BF16_ROWS), NA_CK)
                for j in range(NA_KROWS)]
        kb = jnp.concatenate([k_all[slot, r, :] for r in rows], axis=0)
        vb = jnp.concatenate([v_all[slot, r, :] for r in rows], axis=0)
        s = lax.dot_general(qs, kb, (((1,), (1,)), ((), ())), preferred_element_type=F32)
        s = s + bias_ref[cb]
        m = jnp.max(s, axis=-1, keepdims=True)
        p = jnp.exp(s - m)
        denom = jnp.sum(p, axis=-1, keepdims=True)
        o = jnp.dot(p.astype(BF16), vb, preferred_element_type=F32) / denom
        out = o[0:NA_BQ]
        for h in range(1, N_GROUPS):
            out = jnp.where(head == h, o[h * NA_BQ:(h + 1) * NA_BQ], out)
        for i in range(NA_TR):
            rows = slice(i * GRID_W + c0, i * GRID_W + c0 + NA_CQ)
            o_ref[rows, :] = (out[i * NA_CQ:(i + 1) * NA_CQ]
                              * _silu(gate_ref[rows, :].astype(F32))).astype(BF16)


def _na(l, z, bias):
    _, B, L, _ = z.shape
    nt = L // NA_TQ
    nkb = NA_KROWS // NA_TR
    assert nt % NA_SUB == 0 and nt >= nkb, L
    nwin = min(NA_SUB + nkb - 1, nt)

    def kv(s, d):
        return pl.BlockSpec((None, None, NA_TQ, D_BRANCH),
                            lambda b, i: (s, b, _na_first_block(i, nt, nwin) + d, 0))

    def bias_of(t):
        return pl.BlockSpec(
            (None, None, NA_NCB, N_GROUPS * NA_BQ, NA_BK),
            lambda b, i: (l, _tile_variant(i * NA_SUB + t, nt), 0, 0, 0))

    bias_inner = pl.BlockSpec((None, None, NA_NCB, N_GROUPS * NA_BQ, NA_BK),
                              lambda b, i: (l, 1, 0, 0, 0), pipeline_mode=pl.Buffered(1))
    return pl.pallas_call(
        functools.partial(_na_kernel, nt=nt, nwin=nwin),
        grid=(B, nt // NA_SUB),
        in_specs=[_slab(ZD_Q, NA_SUB * NA_TQ), _slab(ZD_GATE, NA_SUB * NA_TQ),
                  bias_of(0), bias_inner, bias_of(NA_SUB - 1)]
        + [kv(ZD_K, d) for d in range(nwin)] + [kv(ZD_V, d) for d in range(nwin)],
        out_specs=_rows(NA_SUB * NA_TQ, D_BRANCH),
        out_shape=jax.ShapeDtypeStruct((B, L, D_BRANCH), BF16),
        scratch_shapes=[pltpu.VMEM((2, nwin * NA_TQ, D_BRANCH), BF16),
                        pltpu.VMEM((2, nwin * NA_TQ, D_BRANCH), BF16)],
        compiler_params=_params(2), name="na",
    )(z, z, bias, bias, bias, *([z] * (2 * nwin)))


def _final_kernel(ya_ref, yb_ref, yc_ref, yd_ref, cg_ref, x_ref, gate_ref, w_ref, fg_ref, o_ref):
    y = _mix_out([ya_ref[...], yb_ref[...], _gated(yc_ref[...], cg_ref[...]), yd_ref[...]], w_ref)
    o_ref[...] = _rms(x_ref[...] + gate_ref[...] * y) * fg_ref[...]


def _final(b0, ys, z_prev, x, mod, prm, final_g, tm=TM):
    B, L, _ = x.shape
    last = DEPTH - 1
    return pl.pallas_call(
        _final_kernel,
        grid=(B, L // tm),
        in_specs=[_rows(tm, D_BRANCH)] * 4 + [
            _slab(ZC_GATE, tm), _rows(tm, D_MODEL), _mod_vec(last, 2, b0),
            _of_layer(last, (D_MODEL, D_MODEL)), _const((1, D_MODEL))],
        out_specs=_rows(tm, D_MODEL),
        out_shape=jax.ShapeDtypeStruct((B, L, D_MODEL), F32),
        compiler_params=_params(2), name="final",
    )(*ys, z_prev, x, mod, prm["w_out"], final_g.reshape(1, D_MODEL))


def _trunk(x, b0, mod, prm, final_norm_g):
    ys, z = None, None
    for l in range(DEPTH):
        x, z, ya, yb = _layer(l, b0, ys, z, x, mod, prm)
        yc = _fnet(l, z, prm["fnet_mc"], prm["fnet_ms"])
        yd = _na(l, z, prm["na_bias"])
        ys = (ya, yb, yc, yd)
    return _final(b0, ys, z, x, mod, prm, final_norm_g)


def kernel(x_prompt, x_sample, c_prompt, c_sample, norm_g, w_ada, b_ada, w_in, w_out, pool_w, pool_scale, sgu_norm_g, sgu_w, sgu_b, fnet_w, na_rpb, final_norm_g):
    nb_p, nb_s = c_prompt.shape[0], c_sample.shape[0]
    c_all = jnp.concatenate(
        [c_prompt, c_sample, jnp.zeros((C_PAD - nb_p - nb_s, D_MODEL), F32)], axis=0)
    mod = _adaln(c_all, w_ada, b_ada)
    mod = jnp.transpose(mod.reshape(DEPTH, C_PAD, 3, 1, D_MODEL), (0, 2, 1, 3, 4))

    mc, ms = _fnet_fold(_block_diag(fnet_w))
    prm = {
        "norm_g": norm_g.reshape(DEPTH, 1, D_MODEL),
        "sgu_g": sgu_norm_g.reshape(DEPTH, 1, D_BRANCH),
        "pool_scale": pool_scale.reshape(DEPTH, 1, D_BRANCH),
        "w_in": w_in.astype(BF16),
        "w_out": w_out.astype(BF16),
        "pool_w": _block_diag(pool_w).astype(BF16),
        "sgu_w": jnp.swapaxes(sgu_w.reshape(DEPTH, N_GROUPS // 2, 2, CHUNK, CHUNK), 2, 3)
        .reshape(DEPTH, N_GROUPS // 2, CHUNK, 2 * CHUNK).astype(BF16),
        "sgu_b": jnp.repeat(jnp.swapaxes(sgu_b, 1, 2), D_GROUP, axis=2),
        "fnet_mc": mc,
        "fnet_ms": ms,
        "na_bias": _na_bias(na_rpb),
    }
    y_prompt = _trunk(x_prompt, 0, mod, prm, final_norm_g)
    y_sample = _trunk(x_sample, nb_p, mod, prm, final_norm_g)
    return (y_prompt, y_sample)
```

```python
import functools

import numpy as np
import jax
import jax.numpy as jnp
from jax import lax
from jax.experimental import pallas as pl
from jax.experimental.pallas import tpu as pltpu

F32 = jnp.float32
BF16 = jnp.bfloat16
BF16_ROWS = 16

D_MODEL = 1024
DEPTH = 4
D_BRANCH = 256
N_GROUPS = 4
D_GROUP = 64
POOL_WINDOWS = (2, 4, 8, 16)
POOL_HALO = BF16_ROWS
CHUNK = 128
GRID_W = 64
NA_KH = 8
NA_KW = 16
N_IN_SLICES = 11
D_IN = N_IN_SLICES * D_BRANCH
RMS_EPS = 1e-6
LN_EPS = 1e-5
NEG_BIG = -1e30

A_IN, A_GATE, B_U, B_V, B_GATE, C_IN, C_GATE, D_Q, D_K, D_V, D_GATE = range(N_IN_SLICES)
Z_SLICES = (C_IN, C_GATE, D_Q, D_K, D_V, D_GATE)
ZC_IN, ZC_GATE, ZD_Q, ZD_K, ZD_V, ZD_GATE = range(len(Z_SLICES))

NA_TR = 4
NA_KROWS = NA_TR + NA_KH
NA_TQ = NA_TR * GRID_W
NA_SUB = 8
NA_CQ = 16
NA_NCB = GRID_W // NA_CQ
NA_CK = 2 * NA_KW
NA_BQ = NA_TR * NA_CQ
NA_BK = NA_KROWS * NA_CK
NA_JG = 128 // NA_CK
C_PAD = 16
DFT_T = BF16_ROWS
DFT_K = 256
DFT_TOKENS = 2048
TM = 1024
LAYER_ROWS = 256

VMEM_LIMIT = 56 * 1024 * 1024


def _params(n_axes, vmem=VMEM_LIMIT):
    return pltpu.CompilerParams(dimension_semantics=("arbitrary",) * n_axes,
                                vmem_limit_bytes=vmem)


def _silu(x):
    return x / (1.0 + jnp.exp(-x))


def _gelu(x):
    return x * (0.5 * (1.0 + jnp.tanh(np.sqrt(2.0 / np.pi) * (x + 0.044715 * (x * x * x)))))


def _rms(x):
    return x * lax.rsqrt(jnp.mean(x * x, axis=-1, keepdims=True) + RMS_EPS)


def _tile_variant(i, n):
    return jnp.where(i == 0, 0, jnp.where(i == n - 1, 2, 1))


def _const(shape):
    zeros = (0,) * len(shape)
    return pl.BlockSpec(shape, lambda b, i: zeros, pipeline_mode=pl.Buffered(1))


def _rows(t, width):
    return pl.BlockSpec((None, t, width), lambda b, i: (b, i, 0))


def _slab(s, t):
    return pl.BlockSpec((None, None, t, D_BRANCH), lambda b, i: (s, b, i, 0))


def _adaln_kernel(c_ref, w_ref, b_ref, o_ref):
    s = _silu(c_ref[...])
    o_ref[...] = jnp.dot(s, w_ref[...], precision=lax.Precision.HIGHEST,
                         preferred_element_type=F32) + b_ref[...]


def _adaln(c_all, w_ada, b_ada):
    return pl.pallas_call(
        _adaln_kernel,
        grid=(DEPTH, 3),
        in_specs=[pl.BlockSpec((C_PAD, D_MODEL), lambda l, n: (0, 0)),
                  pl.BlockSpec((None, D_MODEL, D_MODEL), lambda l, n: (l, 0, n)),
                  pl.BlockSpec((None, 1, D_MODEL), lambda l, n: (l, 0, n))],
        out_specs=pl.BlockSpec((None, C_PAD, D_MODEL), lambda l, n: (l, 0, n)),
        out_shape=jax.ShapeDtypeStruct((DEPTH, C_PAD, 3 * D_MODEL), F32),
        compiler_params=_params(2), name="adaln",
    )(c_all, w_ada, b_ada.reshape(DEPTH, 1, 3 * D_MODEL))


@functools.lru_cache(maxsize=None)
def _pool_tables(tp):
    r = np.arange(CHUNK)[:, None]
    c = np.arange(CHUNK + 2 * POOL_HALO)[None, :]
    band = np.stack([(c >= r + POOL_HALO - w // 2) & (c < r + POOL_HALO + w // 2)
                     for w in POOL_WINDOWS]).astype(np.float32)
    t = np.arange(tp)[:, None]
    w = np.repeat(np.array(POOL_WINDOWS), D_GROUP)[None, :]
    head = np.minimum(t + w // 2, tp + w) - np.maximum(t - w // 2, 0)
    tail = np.minimum(t + w // 2, tp) - np.maximum(t - w // 2, -w)
    inv = np.stack([1.0 / head, 1.0 / np.broadcast_to(w, head.shape), 1.0 / tail])
    return band, inv.astype(np.float32)


def _gated(y, gate):
    return (y.astype(F32) * _silu(gate.astype(F32))).astype(BF16)


def _mix_out(ys, w_ref):
    y = jnp.dot(ys[0], w_ref[0:D_BRANCH, :], preferred_element_type=F32)
    for n in range(1, len(ys)):
        y = y + jnp.dot(ys[n], w_ref[n * D_BRANCH:(n + 1) * D_BRANCH, :],
                        preferred_element_type=F32)
    return y


def _modulate(x, g_ref, shift_ref, scl_ref):
    return ((_rms(x) * g_ref[...]) * (1.0 + scl_ref[...]) + shift_ref[...]).astype(BF16)


def _sgu_body(u, v, gate, g_ref, w_ref, b_ref, o_ref):
    t = u.shape[0]
    u = _gelu(u)
    v = _gelu(v)
    mu = jnp.mean(v, axis=-1, keepdims=True)
    d = v - mu
    var = jnp.mean(d * d, axis=-1, keepdims=True)
    vn = (d * lax.rsqrt(var + LN_EPS) * g_ref[...]).astype(BF16)
    gate = _silu(gate)
    head = lax.broadcasted_iota(jnp.int32, (CHUNK, D_BRANCH), 1) // D_GROUP
    zero = jnp.zeros((CHUNK, D_BRANCH), BF16)
    for n in range(t // CHUNK):
        rows = slice(n * CHUNK, (n + 1) * CHUNK)
        vc = vn[rows]
        s = None
        for p in range(N_GROUPS // 2):
            rhs = jnp.concatenate([jnp.where(head == 2 * p, vc, zero),
                                   jnp.where(head == 2 * p + 1, vc, zero)], axis=0)
            sp = jnp.dot(w_ref[p], rhs, preferred_element_type=F32)
            s = sp if s is None else s + sp
        o_ref[rows, :] = ((u[rows] * (s + b_ref[...])) * gate[rows]).astype(BF16)


def _pool_body(ext_ref, gate, band_ref, inv_ref, w_ref, scale_ref, o_ref):
    t = gate.shape[0]
    H = POOL_HALO
    gate = _silu(gate)
    group = lax.broadcasted_iota(jnp.int32, (CHUNK, D_BRANCH), 1) // D_GROUP
    for m in range(t // CHUNK):
        rows = slice(m * CHUNK, (m + 1) * CHUNK)
        e = ext_ref[m * CHUNK:(m + 1) * CHUNK + 2 * H, :]
        s = jnp.dot(band_ref[0], e, preferred_element_type=F32)
        for g in range(1, N_GROUPS):
            s = jnp.where(group == g, jnp.dot(band_ref[g], e, preferred_element_type=F32), s)
        p = s * inv_ref[rows, :] - e[H:H + CHUNK].astype(F32)
        y = jnp.dot(p.astype(BF16), w_ref[...], preferred_element_type=F32) * scale_ref[...]
        o_ref[rows, :] = (y * gate[rows]).astype(BF16)


def _layer_kernel(*refs, tm, first):
    n_stream = 2 if first else 14
    stream, rest = refs[:n_stream], refs[n_stream:]
    (g_ref, shift_ref, scl_ref, win_ref, sg_ref, sw_ref, sb_ref,
     band_ref, inv_ref, pw_ref, ps_ref) = rest[:11]
    outs = rest[11:]
    i = pl.program_id(1)
    n = pl.num_programs(1)
    H = POOL_HALO
    ext_ref, carry_ref = outs[-2:]
    slot_prev = (i + 1) % 2
    slot_this = i % 2

    @pl.when(i == 0)
    def _():
        carry_ref[slot_prev] = jnp.zeros((H, D_BRANCH), BF16)

    if first:
        x_ref, xh_ref = stream
        z_ref, ya_ref, yb_ref = outs[:3]
        h_ext = _modulate(jnp.concatenate([x_ref[...], xh_ref[...]], axis=0),
                          g_ref, shift_ref, scl_ref)
    else:
        ys, x_ref, ysh, xh_ref = stream[0:4], stream[4], stream[5:9], stream[9]
        cg_ref, cgh_ref, gprev_ref, wout_ref = stream[10:14]
        xo_ref, z_ref, ya_ref, yb_ref = outs[:4]
        ys_ext = [jnp.concatenate([m[...], mh[...]], axis=0) for m, mh in zip(ys, ysh)]
        ys_ext[2] = _gated(ys_ext[2], jnp.concatenate([cg_ref[...], cgh_ref[...]], axis=0))
        h_parts = []
        for r0 in range(0, tm, LAYER_ROWS):
            r1 = r0 + LAYER_ROWS
            if r1 == tm:
                r1 = tm + H
                x_in = jnp.concatenate([x_ref[r0:tm, :], xh_ref[...]], axis=0)
            else:
                x_in = x_ref[r0:r1, :]
            x = x_in + gprev_ref[...] * _mix_out([m[r0:r1] for m in ys_ext], wout_ref)
            xo_ref[r0:min(r1, tm), :] = x[0:min(r1, tm) - r0]
            h_parts.append(_modulate(x, g_ref, shift_ref, scl_ref))
        h_ext = jnp.concatenate(h_parts, axis=0)

    h = h_ext[0:tm]
    zl = jnp.dot(h_ext, win_ref[:, 0:Z_SLICES[0] * D_BRANCH], preferred_element_type=F32)

    def local(s):
        return zl[0:tm, s * D_BRANCH:(s + 1) * D_BRANCH]

    _sgu_body(local(B_U), local(B_V), local(B_GATE), sg_ref, sw_ref, sb_ref, yb_ref)

    a_ext = zl[:, A_IN * D_BRANCH:(A_IN + 1) * D_BRANCH].astype(BF16)
    a = a_ext[0:tm]
    a_next = a_ext[tm:tm + H]
    ext_ref[0:H, :] = carry_ref[slot_prev]
    ext_ref[H:H + tm, :] = a
    ext_ref[H + tm:2 * H + tm, :] = jnp.where(i < n - 1, a_next, jnp.zeros_like(a_next))
    carry_ref[slot_this] = a[tm - H:tm]
    _pool_body(ext_ref, local(A_GATE), band_ref, inv_ref, pw_ref, ps_ref, ya_ref)

    zg = jnp.dot(h, win_ref[:, Z_SLICES[0] * D_BRANCH:], preferred_element_type=F32)
    for k in range(len(Z_SLICES)):
        z_ref[k] = zg[:, k * D_BRANCH:(k + 1) * D_BRANCH].astype(BF16)


def _of_layer(l, shape):
    zeros = (0,) * len(shape)
    return pl.BlockSpec((None, *shape), lambda b, i: (l, *zeros), pipeline_mode=pl.Buffered(1))


def _mod_vec(l, n, b0):
    return pl.BlockSpec((None, None, None, 1, D_MODEL), lambda b, i: (l, n, b0 + b, 0, 0))


def _layer(l, b0, ys, z_prev, x, mod, prm, tm=TM):
    B, L, _ = x.shape
    first = ys is None
    H = POOL_HALO
    nt = L // tm
    nh = L // H
    band, inv = _pool_tables(tm)

    def halo(width):
        return pl.BlockSpec((None, H, width),
                            lambda b, i: (b, jnp.minimum((i + 1) * (tm // H), nh - 1), 0))

    if first:
        stream_specs = [_rows(tm, D_MODEL), halo(D_MODEL)]
        stream_args = [x, x]
    else:
        gate_halo = pl.BlockSpec(
            (None, None, H, D_BRANCH),
            lambda b, i: (ZC_GATE, b, jnp.minimum((i + 1) * (tm // H), nh - 1), 0))
        stream_specs = ([_rows(tm, D_BRANCH)] * 4 + [_rows(tm, D_MODEL)] + [halo(D_BRANCH)] * 4
                        + [halo(D_MODEL), _slab(ZC_GATE, tm), gate_halo, _mod_vec(l - 1, 2, b0),
                           _of_layer(l - 1, (D_MODEL, D_MODEL))])
        stream_args = [*ys, x, *ys, x, z_prev, z_prev, mod, prm["w_out"]]
    param_specs = [_of_layer(l, (1, D_MODEL)), _mod_vec(l, 0, b0), _mod_vec(l, 1, b0),
                   _of_layer(l, (D_MODEL, D_IN)),
                   _of_layer(l, (1, D_BRANCH)), _of_layer(l, (N_GROUPS // 2, CHUNK, 2 * CHUNK)),
                   _of_layer(l, (CHUNK, D_BRANCH)),
                   _const(band.shape),
                   pl.BlockSpec((None, tm, D_BRANCH), lambda b, i: (_tile_variant(i, nt), 0, 0)),
                   _of_layer(l, (D_BRANCH, D_BRANCH)), _of_layer(l, (1, D_BRANCH))]
    param_args = [prm["norm_g"], mod, mod, prm["w_in"],
                  prm["sgu_g"], prm["sgu_w"], prm["sgu_b"],
                  jnp.asarray(band).astype(BF16), jnp.asarray(inv), prm["pool_w"],
                  prm["pool_scale"]]
    nz = len(Z_SLICES)
    out_specs = [pl.BlockSpec((nz, None, tm, D_BRANCH), lambda b, i: (0, b, i, 0)),
                 _rows(tm, D_BRANCH), _rows(tm, D_BRANCH)]
    out_shape = [jax.ShapeDtypeStruct((nz, B, L, D_BRANCH), BF16),
                 jax.ShapeDtypeStruct((B, L, D_BRANCH), BF16),
                 jax.ShapeDtypeStruct((B, L, D_BRANCH), BF16)]
    if not first:
        out_specs = [_rows(tm, D_MODEL)] + out_specs
        out_shape = [jax.ShapeDtypeStruct((B, L, D_MODEL), F32)] + out_shape
    res = pl.pallas_call(
        functools.partial(_layer_kernel, tm=tm, first=first),
        grid=(B, nt),
        in_specs=stream_specs + param_specs,
        out_specs=out_specs,
        out_shape=out_shape,
        scratch_shapes=[pltpu.VMEM((tm + 2 * H, D_BRANCH), BF16),
                        pltpu.VMEM((2, H, D_BRANCH), BF16)],
        compiler_params=_params(2), name="layer_first" if first else "layer",
    )(*stream_args, *param_args)
    return (x, *res) if first else tuple(res)


def _dft_split(L):
    n2 = 128
    return L // n2, n2


def _dft_group(n1):
    return DFT_K // (2 * n1)


@functools.lru_cache(maxsize=None)
def _fnet_tables(L):
    n1, n2 = _dft_split(L)
    k1 = np.arange(n1, dtype=np.int64)
    n = np.arange(n2, dtype=np.int64)[:, None, None] + n2 * np.arange(n1, dtype=np.int64)[None, None, :]
    ang = 2.0 * np.pi * ((k1[None, :, None] * n) % L).astype(np.float64) / L
    gc, gs = np.cos(ang), np.sin(ang)
    gg = np.concatenate([np.concatenate([gc, -gs], axis=2),
                         np.concatenate([-gs, -gc], axis=2)], axis=1)
    jb = _dft_group(n1)
    grouped = np.zeros((n2 // jb, jb, 2 * n1, jb, 2 * n1))
    for j in range(jb):
        grouped[:, j, :, j, :] = gg[j::jb]
    gg = grouped.reshape(n2 // jb, DFT_K, DFT_K)
    kk = np.arange(n2, dtype=np.int64)
    ang2 = 2.0 * np.pi * ((kk[:, None] * kk[None, :]) % n2).astype(np.float64) / n2
    cs2 = np.concatenate([np.cos(ang2), np.sin(ang2)], axis=1)
    return gg.astype(np.float32), cs2.astype(np.float32)


@functools.lru_cache(maxsize=None)
def _channel_tables():
    c = np.arange(D_GROUP, dtype=np.int64)
    ang = 2.0 * np.pi * ((c[:, None] * c[None, :]) % D_GROUP).astype(np.float64) / D_GROUP
    eye = np.eye(N_GROUPS)
    return (np.kron(eye, np.cos(ang)).astype(np.float32),
            np.kron(eye, np.sin(ang)).astype(np.float32))


def _block_diag(w):
    eye = jnp.eye(N_GROUPS, dtype=w.dtype)
    out = jnp.einsum('...gcd,gh->...gchd', w, eye)
    return out.reshape(*w.shape[:-3], D_BRANCH, D_BRANCH)


def _fold_kernel(bdc_ref, bds_ref, w_ref, mc_ref, ms_ref):
    w = w_ref[...]
    hi = lax.Precision.HIGHEST
    mc_ref[...] = jnp.dot(bdc_ref[...], w, precision=hi, preferred_element_type=F32).astype(BF16)
    ms_ref[...] = jnp.dot(bds_ref[...], w, precision=hi, preferred_element_type=F32).astype(BF16)


def _fnet_fold(fnet_w_bd):
    bdc, bds = _channel_tables()
    full = pl.BlockSpec((D_BRANCH, D_BRANCH), lambda l: (0, 0))
    per = pl.BlockSpec((None, D_BRANCH, D_BRANCH), lambda l: (l, 0, 0))
    shp = jax.ShapeDtypeStruct((DEPTH, D_BRANCH, D_BRANCH), BF16)
    return pl.pallas_call(
        _fold_kernel, grid=(DEPTH,), in_specs=[full, full, per], out_specs=[per, per],
        out_shape=[shp, shp], compiler_params=_params(1), name="fnet_fold",
    )(jnp.asarray(bdc), jnp.asarray(bds), fnet_w_bd)


def _fnet1_kernel(x_ref, gg_ref, mc_ref, ms_ref, a_ref, *, n1, t):
    jb = _dft_group(n1)
    xt = jnp.swapaxes(x_ref[...].astype(F32), 0, 1)
    x2 = xt.reshape(t * n1, D_BRANCH).astype(BF16)
    u = jnp.dot(x2, mc_ref[...], preferred_element_type=F32)
    v = jnp.dot(x2, ms_ref[...], preferred_element_type=F32)
    groups = []
    for g in range(t // jb):
        parts = []
        for j in range(g * jb, (g + 1) * jb):
            parts += [u[j * n1:(j + 1) * n1], v[j * n1:(j + 1) * n1]]
        uv = jnp.concatenate(parts, axis=0).astype(BF16)
        a = jnp.dot(gg_ref[g], uv, preferred_element_type=F32)
        groups.append(a.reshape(jb, 2 * n1, D_BRANCH))
    a_ref[...] = jnp.swapaxes(jnp.concatenate(groups, axis=0), 0, 1).astype(BF16)


def _fnet2_kernel(ar_ref, ai_ref, cs_ref, o_ref, *, norm):
    ys = []
    for j in range(ar_ref.shape[0]):
        a = jnp.concatenate([ar_ref[j], ai_ref[j]], axis=0)
        ys.append(jnp.dot(cs_ref[...], a, preferred_element_type=F32))
    y = jnp.swapaxes(jnp.stack(ys, axis=0), 0, 1)
    o_ref[...] = (y * norm).astype(BF16)


def _fnet(l, z, mc, ms):
    nz, B, L, _ = z.shape
    n1, n2 = _dft_split(L)
    T = DFT_T
    assert L == n1 * n2 and n1 % T == 0 and DFT_K % (2 * n1) == 0, L
    gg, cs2 = _fnet_tables(L)
    gg, cs2 = jnp.asarray(gg).astype(BF16), jnp.asarray(cs2).astype(BF16)
    full = _of_layer(l, (D_BRANCH, D_BRANCH))
    t1 = min(n2, max(T, DFT_TOKENS // n1))
    a = pl.pallas_call(
        functools.partial(_fnet1_kernel, n1=n1, t=t1),
        grid=(B, n2 // t1),
        in_specs=[pl.BlockSpec((None, None, n1, t1, D_BRANCH), lambda b, i: (ZC_IN, b, 0, i, 0)),
                  pl.BlockSpec((t1 // _dft_group(n1), DFT_K, DFT_K), lambda b, i: (i, 0, 0)),
                  full, full],
        out_specs=pl.BlockSpec((None, 2 * n1, t1, D_BRANCH), lambda b, i: (b, 0, i, 0)),
        out_shape=jax.ShapeDtypeStruct((B, 2 * n1, n2, D_BRANCH), BF16),
        compiler_params=_params(2), name="fnet_stage1",
    )(z.reshape(nz, B, n1, n2, D_BRANCH), gg, mc, ms)

    t2 = min(n1, DFT_TOKENS * 2 // n2)
    nk = n1 // t2
    y = pl.pallas_call(
        functools.partial(_fnet2_kernel, norm=float(1.0 / np.sqrt(L * D_GROUP))),
        grid=(B, nk),
        in_specs=[pl.BlockSpec((None, t2, n2, D_BRANCH), lambda b, i: (b, i, 0, 0)),
                  pl.BlockSpec((None, t2, n2, D_BRANCH), lambda b, i: (b, nk + i, 0, 0)),
                  pl.BlockSpec((n2, 2 * n2), lambda b, i: (0, 0))],
        out_specs=pl.BlockSpec((None, n2, t2, D_BRANCH), lambda b, i: (b, 0, i, 0)),
        out_shape=jax.ShapeDtypeStruct((B, n2, n1, D_BRANCH), BF16),
        compiler_params=_params(2), name="fnet_stage2",
    )(a, a, cs2)
    return y.reshape(B, L, D_BRANCH)


def _na_key_col0(cb):
    return int(np.clip(cb * NA_CQ - NA_KW // 2, 0, GRID_W - NA_CK))


@functools.lru_cache(maxsize=None)
def _na_tables():
    i = np.arange(NA_TR)[:, None]
    j = np.arange(NA_KROWS)[None, :]
    row_sel = np.zeros((3, NA_TR, NA_KROWS, 2 * NA_KH - 1), np.float32)
    row_ok = np.zeros((3, NA_TR, NA_KROWS), bool)
    for var, (base, rel) in enumerate(((0, np.zeros(NA_TR, int)), (-NA_TR, np.arange(NA_TR)),
                                       (-NA_KH, np.full(NA_TR, NA_KROWS - NA_KH)))):
        ok = (j >= rel[:, None]) & (j < rel[:, None] + NA_KH)
        dr = base + j - i + NA_KH - 1
        for ii, jj in zip(*np.nonzero(ok)):
            row_sel[var, ii, jj, dr[ii, jj]] = 1.0
        row_ok[var] = ok
    col_sel = np.zeros((NA_NCB, NA_CQ, NA_CK, 2 * NA_KW - 1), np.float32)
    col_ok = np.zeros((NA_NCB, NA_CQ, NA_CK), bool)
    for cb in range(NA_NCB):
        for cq in range(NA_CQ):
            c = cb * NA_CQ + cq
            cst = int(np.clip(c - NA_KW // 2, 0, GRID_W - NA_KW))
            for kl in range(NA_CK):
                kc = _na_key_col0(cb) + kl
                if cst <= kc < cst + NA_KW:
                    col_sel[cb, cq, kl, kc - c + NA_KW - 1] = 1.0
                    col_ok[cb, cq, kl] = True
    ok = row_ok[:, None, :, None, :, None] & col_ok[None, :, None, :, None, :]
    mask = np.where(ok, 0.0, NEG_BIG).astype(np.float32).reshape(3, NA_NCB, 1, NA_BQ, NA_BK)
    nb = 2 * NA_KW - 1
    col_sel_g = np.zeros((NA_JG, nb, NA_NCB, NA_CQ, NA_JG, NA_CK), np.float32)
    for j in range(NA_JG):
        col_sel_g[j, :, :, :, j, :] = np.transpose(col_sel, (3, 0, 1, 2))
    return row_sel, col_sel_g.reshape(NA_JG * nb, NA_NCB * NA_CQ * NA_JG * NA_CK), mask


def _na_bias(na_rpb):
    row_sel, col_sel_g, mask = _na_tables()
    hi = lax.Precision.HIGHEST
    ng = NA_KROWS // NA_JG
    b = jnp.einsum('vija,lhab->lvhijb', row_sel, na_rpb, precision=hi)
    b = b.reshape(DEPTH * 3 * N_GROUPS * NA_TR, ng, col_sel_g.shape[0])
    b = jnp.einsum('rgx,xn->rgn', b, col_sel_g, precision=hi)
    b = b.reshape(DEPTH, 3, N_GROUPS, NA_TR, ng, NA_NCB, NA_CQ, NA_JG * NA_CK)
    b = jnp.transpose(b, (0, 1, 5, 2, 3, 6, 4, 7))
    b = b.reshape(DEPTH, 3, NA_NCB, N_GROUPS, NA_BQ, NA_BK) + mask
    b = b.reshape(DEPTH, 3, NA_NCB, N_GROUPS * NA_BQ, NA_BK)
    return jnp.swapaxes(b, -1, -2)


def _na_first_block(i, nt, nwin):
    return jnp.clip(i * NA_SUB - 1, 0, nt - nwin)


def _na_kernel(*refs, nt, nwin):
    q_ref, gate_ref = refs[:2]
    k_refs, v_refs = refs[5:5 + nwin], refs[5 + nwin:5 + 2 * nwin]
    o_ref, k_all, v_all = refs[-3:]
    i = pl.program_id(1)
    nkb = NA_KROWS // NA_TR
    n_rows = nwin * NA_TQ
    for d in range(nwin):
        k_all[0, d * NA_TQ:(d + 1) * NA_TQ, :] = k_refs[d][...]
        v_all[0, d * NA_TQ:(d + 1) * NA_TQ, :] = v_refs[d][...]
    half = BF16_ROWS // 2
    k_all[1, 0:n_rows - BF16_ROWS, :] = k_all[0][half:n_rows - half, :]
    v_all[1, 0:n_rows - BF16_ROWS, :] = v_all[0][half:n_rows - half, :]
    first_block = _na_first_block(i, nt, nwin)
    for t in range(NA_SUB):
        bias_ref = refs[2 + (0 if t == 0 else 2 if t == NA_SUB - 1 else 1)]
        row0 = (jnp.clip(i * NA_SUB + t - 1, 0, nt - nkb) - first_block) * NA_TQ
        _na_tile(q_ref, gate_ref, t * NA_TQ, k_all, v_all, row0, bias_ref, o_ref)


def _na_tile(q_ref, gate_ref, q0, k_all, v_all, key0, bias_ref, o_ref):
    head = lax.broadcasted_iota(jnp.int32, (NA_BQ, D_BRANCH), 1) // D_GROUP
    for cb in range(NA_NCB):
        c0 = q0 + cb * NA_CQ
        kc0 = _na_key_col0(cb)
        q = jnp.concatenate([q_ref[i * GRID_W + c0:i * GRID_W + c0 + NA_CQ, :]
                             for i in range(NA_TR)], axis=0)
        q = q * jnp.asarray(D_GROUP ** -0.5, BF16)
        zero = jnp.zeros_like(q)
        qs = jnp.concatenate([jnp.where(head == h, q, zero) for h in range(N_GROUPS)], axis=0)
        slot = (kc0 % BF16_ROWS) // (BF16_ROWS // 2)
        col = kc0 - slot * (BF16_ROWS // 2)
        rows = [pl.ds(pl.multiple_of(key0 + j * GRID_W + col, BF16_ROWS), NA_CK)
                for j in range(NA_KROWS)]
        kb = jnp.concatenate([k_all[slot, r, :] for r in rows], axis=0)
        vb = jnp.concatenate([v_all[slot, r, :] for r in rows], axis=0)
        s = lax.dot_general(kb, qs, (((1,), (1,)), ((), ())), preferred_element_type=F32)
        s = s + bias_ref[cb]
        m = jnp.max(s, axis=0, keepdims=True)
        p = jnp.exp(s - m)
        p = (p / jnp.sum(p, axis=0, keepdims=True)).astype(BF16)
        o = lax.dot_general(p, vb, (((0,), (0,)), ((), ())),
                            preferred_element_type=F32)
        out = o[0:NA_BQ]
        for h in range(1, N_GROUPS):
            out = jnp.where(head == h, o[h * NA_BQ:(h + 1) * NA_BQ], out)
        for i in range(NA_TR):
            rows = slice(i * GRID_W + c0, i * GRID_W + c0 + NA_CQ)
            o_ref[rows, :] = (out[i * NA_CQ:(i + 1) * NA_CQ]
                              * _silu(gate_ref[rows, :].astype(F32))).astype(BF16)


def _na(l, z, bias):
    _, B, L, _ = z.shape
    nt = L // NA_TQ
    nkb = NA_KROWS // NA_TR
    assert nt % NA_SUB == 0 and nt >= nkb, L
    nwin = min(NA_SUB + nkb - 1, nt)

    def kv(s, d):
        return pl.BlockSpec((None, None, NA_TQ, D_BRANCH),
                            lambda b, i: (s, b, _na_first_block(i, nt, nwin) + d, 0))

    def bias_of(t):
        return pl.BlockSpec(
            (None, None, NA_NCB, NA_BK, N_GROUPS * NA_BQ),
            lambda b, i: (l, _tile_variant(i * NA_SUB + t, nt), 0, 0, 0))

    bias_inner = pl.BlockSpec((None, None, NA_NCB, NA_BK, N_GROUPS * NA_BQ),
                              lambda b, i: (l, 1, 0, 0, 0), pipeline_mode=pl.Buffered(1))
    return pl.pallas_call(
        functools.partial(_na_kernel, nt=nt, nwin=nwin),
        grid=(B, nt // NA_SUB),
        in_specs=[_slab(ZD_Q, NA_SUB * NA_TQ), _slab(ZD_GATE, NA_SUB * NA_TQ),
                  bias_of(0), bias_inner, bias_of(NA_SUB - 1)]
        + [kv(ZD_K, d) for d in range(nwin)] + [kv(ZD_V, d) for d in range(nwin)],
        out_specs=_rows(NA_SUB * NA_TQ, D_BRANCH),
        out_shape=jax.ShapeDtypeStruct((B, L, D_BRANCH), BF16),
        scratch_shapes=[pltpu.VMEM((2, nwin * NA_TQ, D_BRANCH), BF16),
                        pltpu.VMEM((2, nwin * NA_TQ, D_BRANCH), BF16)],
        compiler_params=_params(2), name="na",
    )(z, z, bias, bias, bias, *([z] * (2 * nwin)))


def _final_kernel(ya_ref, yb_ref, yc_ref, yd_ref, cg_ref, x_ref, gate_ref, w_ref, fg_ref, o_ref):
    y = _mix_out([ya_ref[...], yb_ref[...], _gated(yc_ref[...], cg_ref[...]), yd_ref[...]], w_ref)
    o_ref[...] = _rms(x_ref[...] + gate_ref[...] * y) * fg_ref[...]


def _final(b0, ys, z_prev, x, mod, prm, final_g, tm=TM):
    B, L, _ = x.shape
    last = DEPTH - 1
    return pl.pallas_call(
        _final_kernel,
        grid=(B, L // tm),
        in_specs=[_rows(tm, D_BRANCH)] * 4 + [
            _slab(ZC_GATE, tm), _rows(tm, D_MODEL), _mod_vec(last, 2, b0),
            _of_layer(last, (D_MODEL, D_MODEL)), _const((1, D_MODEL))],
        out_specs=_rows(tm, D_MODEL),
        out_shape=jax.ShapeDtypeStruct((B, L, D_MODEL), F32),
        compiler_params=_params(2), name="final",
    )(*ys, z_prev, x, mod, prm["w_out"], final_g.reshape(1, D_MODEL))


def _trunk(x, b0, mod, prm, final_norm_g):
    ys, z = None, None
    for l in range(DEPTH):
        x, z, ya, yb = _layer(l, b0, ys, z, x, mod, prm)
        yc = _fnet(l, z, prm["fnet_mc"], prm["fnet_ms"])
        yd = _na(l, z, prm["na_bias"])
        ys = (ya, yb, yc, yd)
    return _final(b0, ys, z, x, mod, prm, final_norm_g)


def kernel(x_prompt, x_sample, c_prompt, c_sample, norm_g, w_ada, b_ada, w_in, w_out, pool_w, pool_scale, sgu_norm_g, sgu_w, sgu_b, fnet_w, na_rpb, final_norm_g):
    nb_p, nb_s = c_prompt.shape[0], c_sample.shape[0]
    c_all = jnp.concatenate(
        [c_prompt, c_sample, jnp.zeros((C_PAD - nb_p - nb_s, D_MODEL), F32)], axis=0)
    mod = _adaln(c_all, w_ada, b_ada)
    mod = jnp.transpose(mod.reshape(DEPTH, C_PAD, 3, 1, D_MODEL), (0, 2, 1, 3, 4))

    mc, ms = _fnet_fold(_block_diag(fnet_w))
    prm = {
        "norm_g": norm_g.reshape(DEPTH, 1, D_MODEL),
        "sgu_g": sgu_norm_g.reshape(DEPTH, 1, D_BRANCH),
        "pool_scale": pool_scale.reshape(DEPTH, 1, D_BRANCH),
        "w_in": w_in.astype(BF16),
        "w_out": w_out.astype(BF16),
        "pool_w": _block_diag(pool_w).astype(BF16),
        "sgu_w": jnp.swapaxes(sgu_w.reshape(DEPTH, N_GROUPS // 2, 2, CHUNK, CHUNK), 2, 3)
        .reshape(DEPTH, N_GROUPS // 2, CHUNK, 2 * CHUNK).astype(BF16),
        "sgu_b": jnp.repeat(jnp.swapaxes(sgu_b, 1, 2), D_GROUP, axis=2),
        "fnet_mc": mc,
        "fnet_ms": ms,
        "na_bias": _na_bias(na_rpb),
    }
    y_prompt = _trunk(x_prompt, 0, mod, prm, final_norm_g)
    y_sample = _trunk(x_sample, nb_p, mod, prm, final_norm_g)
    return (y_prompt, y_sample)
```
